```python
import math
import jax, jax.numpy as jnp
from jax import lax
import numpy as np

D_MODEL = 1024
BATCH = 8
SEQ = 4096
DEPTH = 1

D_FF = 2816
A_WIDTH = 512
A_GROUPS = 8
A_GROUP_DIM = A_WIDTH // A_GROUPS
CHUNK = 128
B_HEADS = 8
B_QK_DIM = 64
B_V_DIM = 2 * B_QK_DIM
B_WIDTH = B_HEADS * B_V_DIM
ROPE_THETA = 500000.0
ROT_DIM = B_QK_DIM // 4
Q_BLOCK = 128
NORM_EPS = 1e-6
LN_EPS = 1e-5
COL_SIZES = (D_MODEL, D_MODEL, 2 * A_WIDTH,
             B_HEADS * 2 * B_QK_DIM, B_HEADS * 2 * B_QK_DIM, B_WIDTH)
IN_COLS = sum(COL_SIZES)
COL_SPLITS = tuple(int(c) for c in np.cumsum(COL_SIZES)[:-1])

kernel_name = 'hybrid_gmlp_diffattn_macaron'


def rms_norm(x, gain, eps=NORM_EPS):
    xf = x.astype(jnp.float32)
    y = xf * lax.rsqrt(jnp.mean(xf * xf, axis=-1, keepdims=True) + eps)
    return (y * gain.astype(jnp.float32)).astype(x.dtype)


def layer_norm(x, gain, bias, eps=LN_EPS):
    xf = x.astype(jnp.float32)
    mu = jnp.mean(xf, axis=-1, keepdims=True)
    xc = xf - mu
    y = xc * lax.rsqrt(jnp.mean(xc * xc, axis=-1, keepdims=True) + eps)
    return (y * gain.astype(jnp.float32) + bias.astype(jnp.float32)).astype(x.dtype)


def swiglu(h, w_gate, w_up, w_down):
    return (jax.nn.silu(h @ w_gate) * (h @ w_up)) @ w_down


def rope_tables(positions):
    inv = ROPE_THETA ** (-jnp.arange(0, ROT_DIM, 2, dtype=jnp.float32) / ROT_DIM)
    ang = positions.astype(jnp.float32)[..., None] * inv
    return jnp.cos(ang), jnp.sin(ang)


def apply_partial_rope(t, cos, sin):
    half = ROT_DIM // 2
    r1, r2, rest = t[..., :half], t[..., half:ROT_DIM], t[..., ROT_DIM:]
    return jnp.concatenate([r1 * cos - r2 * sin, r2 * cos + r1 * sin, rest], axis=-1)


def gmlp_chunk_mixer(z_uv, ln_g, ln_b, w_s, b_s):
    bsz, seq, _ = z_uv.shape
    z = jax.nn.gelu(z_uv, approximate=False)
    u, v = jnp.split(z, 2, axis=-1)
    v = layer_norm(v, ln_g, ln_b)
    vc = v.reshape(bsz, seq // CHUNK, CHUNK, A_GROUPS, A_GROUP_DIM)
    causal = jnp.tril(jnp.ones((CHUNK, CHUNK), dtype=bool))
    w = jnp.where(causal[None], w_s, jnp.zeros_like(w_s))
    f = jnp.einsum('gts,bcsge->bctge', w, vc) + b_s.T[:, :, None]
    return u * f.reshape(bsz, seq, A_WIDTH)


def diff_attention(q, k, v, positions, q_gain, k_gain, lq1, lk1, lq2, lk2, subln_g, lam_init):
    bsz, seq = q.shape[0], q.shape[1]
    out_dtype = v.dtype
    cos, sin = rope_tables(positions)
    cos, sin = cos[:, :, None, None, :], sin[:, :, None, None, :]
    qf = apply_partial_rope(rms_norm(q.astype(jnp.float32), q_gain), cos, sin)
    kf = apply_partial_rope(rms_norm(k.astype(jnp.float32), k_gain), cos, sin)
    qf = qf * (B_QK_DIM ** -0.5)
    qf = qf.transpose(0, 2, 3, 1, 4)
    kf = kf.transpose(0, 2, 3, 1, 4)
    vf = v.astype(jnp.float32).transpose(0, 2, 1, 3)
    lam = (jnp.exp(jnp.sum(lq1.astype(jnp.float32) * lk1.astype(jnp.float32)))
           - jnp.exp(jnp.sum(lq2.astype(jnp.float32) * lk2.astype(jnp.float32)))
           + lam_init)
    kpos = jnp.arange(seq)

    def block(i):
        start = i * Q_BLOCK
        qb = lax.dynamic_slice_in_dim(qf, start, Q_BLOCK, axis=3)
        s = jnp.einsum('bhmqd,bhmkd->bhmqk', qb, kf)
        qpos = start + jnp.arange(Q_BLOCK)
        mask = kpos[None, :] <= qpos[:, None]
        p = jax.nn.softmax(jnp.where(mask, s, -jnp.inf), axis=-1)
        a = p[:, :, 0] - lam * p[:, :, 1]
        return jnp.einsum('bhqk,bhkd->bhqd', a, vf)

    o = lax.map(block, jnp.arange(seq // Q_BLOCK))
    o = o.transpose(1, 0, 3, 2, 4).reshape(bsz, seq, B_HEADS, B_V_DIM)
    o = rms_norm(o, subln_g) * (1.0 - lam_init)
    return o.reshape(bsz, seq, B_WIDTH).astype(out_dtype)


def setup_inputs(seed: int = 0) -> dict:
    key = jax.random.key(seed)
    ks = jax.random.split(key, 32)
    f32 = jnp.float32
    L = DEPTH

    def nrm(k, shape, scale):
        return jax.random.normal(k, shape, dtype=f32) * scale

    def gain(k, shape):
        return 1.0 + 0.01 * jax.random.normal(k, shape, dtype=f32)

    return {
        'x': jax.random.normal(ks[0], (BATCH, SEQ, D_MODEL), dtype=f32),
        'positions': jnp.broadcast_to(jnp.arange(SEQ, dtype=jnp.int32), (BATCH, SEQ)),
        'ffn1_norm': gain(ks[1], (L, D_MODEL)),
        'ffn1_w_gate': nrm(ks[2], (L, D_MODEL, D_FF), D_MODEL ** -0.5),
        'ffn1_w_up': nrm(ks[3], (L, D_MODEL, D_FF), D_MODEL ** -0.5),
        'ffn1_w_down': nrm(ks[4], (L, D_FF, D_MODEL), D_FF ** -0.5),
        'mix_norm': gain(ks[5], (L, D_MODEL)),
        'w_in': nrm(ks[6], (L, D_MODEL, IN_COLS), D_MODEL ** -0.5),
        'a_ln_gain': gain(ks[7], (L, A_WIDTH)),
        'a_ln_bias': nrm(ks[8], (L, A_WIDTH), 0.01),
        'a_w_s': nrm(ks[9], (L, A_GROUPS, CHUNK, CHUNK), 0.5 * CHUNK ** -0.5),
        'a_b_s': gain(ks[10], (L, A_GROUPS, CHUNK)),
        'a_w_proj': nrm(ks[11], (L, A_WIDTH, D_MODEL), A_WIDTH ** -0.5),
        'b_q_norm': gain(ks[12], (L, B_QK_DIM)),
        'b_k_norm': gain(ks[13], (L, B_QK_DIM)),
        'b_lambda_q1': nrm(ks[14], (L, B_QK_DIM), 0.1),
        'b_lambda_k1': nrm(ks[15], (L, B_QK_DIM), 0.1),
        'b_lambda_q2': nrm(ks[16], (L, B_QK_DIM), 0.1),
        'b_lambda_k2': nrm(ks[17], (L, B_QK_DIM), 0.1),
        'b_subln': gain(ks[18], (L, B_V_DIM)),
        'b_w_proj': nrm(ks[19], (L, B_WIDTH, D_MODEL), B_WIDTH ** -0.5),
        'w_out': nrm(ks[20], (L, D_MODEL, D_MODEL), D_MODEL ** -0.5),
        'ffn2_norm': gain(ks[21], (L, D_MODEL)),
        'ffn2_w_gate': nrm(ks[22], (L, D_MODEL, D_FF), D_MODEL ** -0.5),
        'ffn2_w_up': nrm(ks[23], (L, D_MODEL, D_FF), D_MODEL ** -0.5),
        'ffn2_w_down': nrm(ks[24], (L, D_FF, D_MODEL), D_FF ** -0.5),
    }


def reference(x, positions, ffn1_norm, ffn1_w_gate, ffn1_w_up, ffn1_w_down, mix_norm, w_in,
              a_ln_gain, a_ln_bias, a_w_s, a_b_s, a_w_proj, b_q_norm, b_k_norm,
              b_lambda_q1, b_lambda_k1, b_lambda_q2, b_lambda_k2, b_subln, b_w_proj, w_out,
              ffn2_norm, ffn2_w_gate, ffn2_w_up, ffn2_w_down):
    bsz, seq, _ = x.shape
    for l in range(DEPTH):
        lam_init = 0.8 - 0.6 * math.exp(-0.3 * l)
        x = x + 0.5 * swiglu(rms_norm(x, ffn1_norm[l]), ffn1_w_gate[l], ffn1_w_up[l], ffn1_w_down[l])
        h = rms_norm(x, mix_norm[l])
        z = h @ w_in[l]
        g_a, g_b, z_uv, z_q, z_k, z_v = jnp.split(z, COL_SPLITS, axis=-1)
        y_a = gmlp_chunk_mixer(z_uv, a_ln_gain[l], a_ln_bias[l], a_w_s[l], a_b_s[l])
        q = z_q.reshape(bsz, seq, B_HEADS, 2, B_QK_DIM)
        k = z_k.reshape(bsz, seq, B_HEADS, 2, B_QK_DIM)
        v = z_v.reshape(bsz, seq, B_HEADS, B_V_DIM)
        y_b = diff_attention(q, k, v, positions, b_q_norm[l], b_k_norm[l],
                             b_lambda_q1[l], b_lambda_k1[l], b_lambda_q2[l], b_lambda_k2[l],
                             b_subln[l], lam_init)
        m = jax.nn.sigmoid(g_a) * (y_a @ a_w_proj[l]) + jax.nn.sigmoid(g_b) * (y_b @ b_w_proj[l])
        x = x + m @ w_out[l]
        x = x + 0.5 * swiglu(rms_norm(x, ffn2_norm[l]), ffn2_w_gate[l], ffn2_w_up[l], ffn2_w_down[l])
    return x
```

```python
import functools
import math

import jax
import jax.numpy as jnp
from jax import lax
from jax.experimental import pallas as pl
from jax.experimental.pallas import tpu as pltpu

D_MODEL = 1024
D_FF = 2816
A_WIDTH = 512
A_GROUPS = 8
A_GROUP_DIM = A_WIDTH // A_GROUPS
CHUNK = 128
B_HEADS = 8
B_QK_DIM = 64
B_V_DIM = 2 * B_QK_DIM
B_WIDTH = B_HEADS * B_V_DIM
ROPE_THETA = 500000.0
ROT_DIM = B_QK_DIM // 4
NORM_EPS = 1e-6
LN_EPS = 1e-5

LANES = 128
MXU_DIM = 256
VMEM_LIMIT = 56 * 1024 * 1024

FFN_TM = 512
PROJ_TM = 512
MIX_TM = 512
ATT_TQ = 512
ATT_TK = 512

BF16 = jnp.bfloat16
F32 = jnp.float32


def _dot(a, b):
    return jnp.dot(a, b, preferred_element_type=F32)


def _rms_rows(x, gain, eps):
    ms = jnp.mean(x * x, axis=-1, keepdims=True)
    return x * lax.rsqrt(ms + eps) * gain


def _whole_vmem():
    return pl.BlockSpec(memory_space=pltpu.VMEM)


def _params(n_axes):
    return pltpu.CompilerParams(
        dimension_semantics=("arbitrary",) * n_axes,
        vmem_limit_bytes=VMEM_LIMIT,
    )


def _ffn_kernel(x_ref, gain_ref, wg_ref, wu_ref, wd_ref, o_ref, a_ref):
    x = x_ref[...]
    h = _rms_rows(x, gain_ref[...], NORM_EPS).astype(BF16)
    g = _dot(h, wg_ref[...])
    u = _dot(h, wu_ref[...])
    a_ref[...] = (jax.nn.silu(g) * u).astype(BF16)
    y = _dot(a_ref[...], wd_ref[...])
    o_ref[...] = x + 0.5 * y


def _ffn(x, gain, wg, wu, wd):
    t = x.shape[0]
    row = pl.BlockSpec((FFN_TM, D_MODEL), lambda i: (i, 0))
    return pl.pallas_call(
        _ffn_kernel,
        grid=(t // FFN_TM,),
        in_specs=[row, _whole_vmem(), _whole_vmem(), _whole_vmem(), _whole_vmem()],
        out_specs=row,
        out_shape=jax.ShapeDtypeStruct((t, D_MODEL), F32),
        scratch_shapes=[pltpu.VMEM((FFN_TM, D_FF), BF16)],
        compiler_params=_params(1),
        name="ffn",
    )(x, gain, wg, wu, wd)


def _qk_prep(z, gmat_ref, gain, cos_t, sin_lo, sin_hi, scale):
    zz = (z * z).astype(BF16)
    ms = jnp.concatenate(
        [_dot(zz[:, c:c + MXU_DIM], gmat_ref[...]) for c in range(0, B_WIDTH, MXU_DIM)], axis=1)
    zn = z * lax.rsqrt(ms + NORM_EPS) * gain
    half = ROT_DIM // 2
    up = pltpu.roll(zn, B_WIDTH - half, 1)
    dn = pltpu.roll(zn, half, 1)
    out = zn * cos_t + up * sin_lo + dn * sin_hi
    if scale != 1.0:
        out = out * scale
    return out.astype(BF16)


def _proj_kernel(x_ref, pos_ref, gain_ref, win_ref, lng_ref, lnb_ref, ws_ref, bs_ref, pa_ref,
                 gmat_ref, qg_ref, kg_ref, freq_ref, slo_ref, shi_ref,
                 ma_ref, sgb_ref, q_ref, k_ref, v_ref):
    tm = x_ref.shape[0]
    h = _rms_rows(x_ref[...], gain_ref[...], NORM_EPS).astype(BF16)

    def zcols(i):
        return _dot(h, win_ref[:, i * D_MODEL:(i + 1) * D_MODEL])

    sgb_ref[...] = jax.nn.sigmoid(zcols(1)).astype(BF16)

    zuv = zcols(2)
    zuv = 0.5 * zuv * (1.0 + lax.erf(zuv * (2.0 ** -0.5)))
    u = zuv[:, :A_WIDTH]
    v = zuv[:, A_WIDTH:]
    mu = jnp.mean(v, axis=-1, keepdims=True)
    vc = v - mu
    var = jnp.mean(vc * vc, axis=-1, keepdims=True)
    vn = (vc * lax.rsqrt(var + LN_EPS) * lng_ref[...] + lnb_ref[...]).astype(BF16)

    lane = lax.broadcasted_iota(jnp.int32, (CHUNK, LANES), 1)
    lo_half = lane < A_GROUP_DIM
    row_t = lax.broadcasted_iota(jnp.int32, (CHUNK, 2 * CHUNK), 0)
    col_s = lax.broadcasted_iota(jnp.int32, (CHUNK, 2 * CHUNK), 1) % CHUNK
    causal = col_s <= row_t
    zero = jnp.zeros((), BF16)
    f_rows = []
    for c in range(tm // CHUNK):
        f_cols = []
        for j in range(A_WIDTH // LANES):
            vb = vn[c * CHUNK:(c + 1) * CHUNK, j * LANES:(j + 1) * LANES]
            rhs = jnp.concatenate([jnp.where(lo_half, vb, zero), jnp.where(lo_half, zero, vb)], axis=0)
            wpair = jnp.where(causal, ws_ref[j], zero)
            f_cols.append(_dot(wpair, rhs))
        f_rows.append(jnp.concatenate(f_cols, axis=1) + bs_ref[...])
    f = jnp.concatenate(f_rows, axis=0)
    ya = (u * f).astype(BF16)
    ma_ref[...] = (jax.nn.sigmoid(zcols(0)) * _dot(ya, pa_ref[...])).astype(BF16)

    ang = pos_ref[...].astype(F32) * freq_ref[...]
    cos1 = jnp.cos(ang)
    sin1 = jnp.sin(ang)
    reps = B_WIDTH // LANES
    cos_t = jnp.concatenate([cos1] * reps, axis=1)
    sin_lo = jnp.concatenate([sin1 * slo_ref[...]] * reps, axis=1)
    sin_hi = jnp.concatenate([sin1 * shi_ref[...]] * reps, axis=1)

    q_ref[...] = _qk_prep(zcols(3), gmat_ref, qg_ref[...], cos_t, sin_lo, sin_hi, B_QK_DIM ** -0.5)
    k_ref[...] = _qk_prep(zcols(4), gmat_ref, kg_ref[...], cos_t, sin_lo, sin_hi, 1.0)
    v_ref[...] = zcols(5).astype(BF16)


def _proj(x1, pos, gain, win, lng, lnb, ws_pair, bs_full, pa, gmat, qg, kg, freq, slo, shi):
    t = x1.shape[0]
    row = pl.BlockSpec((PROJ_TM, D_MODEL), lambda i: (i, 0))
    out = jax.ShapeDtypeStruct((t, D_MODEL), BF16)
    return pl.pallas_call(
        _proj_kernel,
        grid=(t // PROJ_TM,),
        in_specs=[row, pl.BlockSpec((PROJ_TM, 1), lambda i: (i, 0))] + [_whole_vmem()] * 13,
        out_specs=[row] * 5,
        out_shape=[out] * 5,
        compiler_params=_params(1),
        name="proj",
    )(x1, pos, gain, win, lng, lnb, ws_pair, bs_full, pa, gmat, qg, kg, freq, slo, shi)


def _attn_kernel(lamp_ref, q_ref, k_ref, v_ref, subg_ref, o_ref, acc1, acc2, m1, m2, *, lam_init):
    qi = pl.program_id(2)
    tq, tk = ATT_TQ, ATT_TK
    q = q_ref[0]
    lane = lax.broadcasted_iota(jnp.int32, (tq, LANES), 1)
    zero = jnp.zeros((), BF16)
    qz1 = jnp.where(lane < B_QK_DIM, q, zero)
    qz2 = jnp.where(lane < B_QK_DIM, zero, q)
    ones = jnp.ones((tk, LANES), BF16)

    for acc, m in ((acc1, m1), (acc2, m2)):
        acc[...] = jnp.zeros_like(acc)
        m[...] = jnp.full_like(m, -jnp.inf)

    def step(j, masked):
        start = pl.multiple_of(j * tk, tk)
        kb = k_ref[0, pl.ds(start, tk), :]
        vext = jnp.concatenate([v_ref[0, pl.ds(start, tk), :], ones], axis=1)
        if masked:
            rows = lax.broadcasted_iota(jnp.int32, (tq, tk), 0)
            cols = lax.broadcasted_iota(jnp.int32, (tq, tk), 1)
            keep = cols <= rows
        for qz, acc, m in ((qz1, acc1, m1), (qz2, acc2, m2)):
            s = lax.dot_general(qz, kb, (((1,), (1,)), ((), ())), preferred_element_type=F32)
            if masked:
                s = jnp.where(keep, s, -jnp.inf)
            m_old = m[...]
            m_new = jnp.maximum(m_old, jnp.max(s, axis=1, keepdims=True))
            alpha = jnp.exp(m_old - m_new)
            p = jnp.exp(s - m_new).astype(BF16)
            acc[...] = alpha * acc[...] + _dot(p, vext)
            m[...] = m_new

    def body(j, carry):
        step(j, False)
        return carry

    lax.fori_loop(0, qi, body, 0)
    step(qi, True)

    lp = lamp_ref[...]
    lam = (jnp.exp(jnp.sum(lp[0:1] * lp[1:2], axis=1, keepdims=True))
           - jnp.exp(jnp.sum(lp[2:3] * lp[3:4], axis=1, keepdims=True)) + lam_init)
    o = (acc1[:, :B_V_DIM] / acc1[:, B_V_DIM:]) - lam * (acc2[:, :B_V_DIM] / acc2[:, B_V_DIM:])
    o = _rms_rows(o, subg_ref[...], NORM_EPS) * (1.0 - lam_init)
    o_ref[0] = o.astype(BF16)


def _attn(lamp, q, k, v, subg, lam_init):
    b, s, _ = q.shape
    qspec = pl.BlockSpec((1, ATT_TQ, LANES), lambda bi, hi, i: (bi, i, hi))
    kvspec = pl.BlockSpec((1, s, LANES), lambda bi, hi, i: (bi, 0, hi))
    return pl.pallas_call(
        functools.partial(_attn_kernel, lam_init=lam_init),
        grid=(b, B_HEADS, s // ATT_TQ),
        in_specs=[_whole_vmem(), qspec, kvspec, kvspec, _whole_vmem()],
        out_specs=qspec,
        out_shape=jax.ShapeDtypeStruct((b, s, B_WIDTH), BF16),
        scratch_shapes=[pltpu.VMEM((ATT_TQ, 2 * LANES), F32), pltpu.VMEM((ATT_TQ, 2 * LANES), F32),
                        pltpu.VMEM((ATT_TQ, 1), F32), pltpu.VMEM((ATT_TQ, 1), F32)],
        compiler_params=_params(3),
        name="attn",
    )(lamp, q, k, v, subg)


def _mixout_kernel(x_ref, ma_ref, sgb_ref, yb_ref, pb_ref, wo_ref, o_ref):
    m = ma_ref[...].astype(F32) + sgb_ref[...].astype(F32) * _dot(yb_ref[...], pb_ref[...])
    o_ref[...] = x_ref[...] + _dot(m.astype(BF16), wo_ref[...])


def _mixout(x1, ma, sgb, yb, pb, wo):
    t = x1.shape[0]
    row = pl.BlockSpec((MIX_TM, D_MODEL), lambda i: (i, 0))
    return pl.pallas_call(
        _mixout_kernel,
        grid=(t // MIX_TM,),
        in_specs=[row, row, row, row, _whole_vmem(), _whole_vmem()],
        out_specs=row,
        out_shape=jax.ShapeDtypeStruct((t, D_MODEL), F32),
        compiler_params=_params(1),
        name="mixout",
    )(x1, ma, sgb, yb, pb, wo)


def _rope_lane_tables():
    lane = jnp.arange(LANES) % B_QK_DIM
    half = ROT_DIM // 2
    inv = ROPE_THETA ** (-jnp.arange(0, ROT_DIM, 2, dtype=F32) / ROT_DIM)
    freq = jnp.where(lane < ROT_DIM, inv[lane % half], 0.0).astype(F32)[None, :]
    slo = jnp.where(lane < half, -1.0, 0.0).astype(F32)[None, :]
    shi = jnp.where((lane >= half) & (lane < ROT_DIM), 1.0, 0.0).astype(F32)[None, :]
    return freq, slo, shi


def kernel(x, positions, ffn1_norm, ffn1_w_gate, ffn1_w_up, ffn1_w_down, mix_norm, w_in, a_ln_gain, a_ln_bias, a_w_s, a_b_s, a_w_proj, b_q_norm, b_k_norm, b_lambda_q1, b_lambda_k1, b_lambda_q2, b_lambda_k2, b_subln, b_w_proj, w_out, ffn2_norm, ffn2_w_gate, ffn2_w_up, ffn2_w_down):
    bsz, seq, _ = x.shape
    depth = w_in.shape[0]
    t = bsz * seq
    xt = x.reshape(t, D_MODEL)
    pos = positions.reshape(t, 1)
    freq, slo, shi = _rope_lane_tables()
    grp = jnp.arange(MXU_DIM) // B_QK_DIM
    gmat = jnp.where(grp[:, None] == grp[None, :], 1.0 / B_QK_DIM, 0.0).astype(BF16)

    for l in range(depth):
        lam_init = 0.8 - 0.6 * math.exp(-0.3 * l)
        row = lambda a: a[l][None, :].astype(F32)
        xt = _ffn(xt, row(ffn1_norm), ffn1_w_gate[l].astype(BF16), ffn1_w_up[l].astype(BF16),
                  ffn1_w_down[l].astype(BF16))
        ws_pair = a_w_s[l].reshape(A_GROUPS // 2, 2, CHUNK, CHUNK).transpose(0, 2, 1, 3)
        ws_pair = ws_pair.reshape(A_GROUPS // 2, CHUNK, 2 * CHUNK).astype(BF16)
        bs_full = jnp.repeat(a_b_s[l].T, A_GROUP_DIM, axis=1).astype(F32)
        tile = lambda a: jnp.tile(a[l].astype(F32), B_WIDTH // B_QK_DIM)[None, :]
        ma, sgb, q, k, v = _proj(
            xt, pos, row(mix_norm), w_in[l].astype(BF16), row(a_ln_gain), row(a_ln_bias), ws_pair, bs_full,
            a_w_proj[l].astype(BF16), gmat, tile(b_q_norm), tile(b_k_norm), freq, slo, shi)
        lamp = jnp.stack([b_lambda_q1[l], b_lambda_k1[l], b_lambda_q2[l], b_lambda_k2[l]]).astype(F32)
        shp = (bsz, seq, B_WIDTH)
        yb = _attn(lamp, q.reshape(shp), k.reshape(shp), v.reshape(shp), row(b_subln), lam_init)
        xt = _mixout(xt, ma, sgb, yb.reshape(t, B_WIDTH), b_w_proj[l].astype(BF16), w_out[l].astype(BF16))
        xt = _ffn(xt, row(ffn2_norm), ffn2_w_gate[l].astype(BF16), ffn2_w_up[l].astype(BF16),
                  ffn2_w_down[l].astype(BF16))
    return xt.reshape(bsz, seq, D_MODEL)
```

```python
import functools
import math

import jax
import jax.numpy as jnp
from jax import lax
from jax.experimental import pallas as pl
from jax.experimental.pallas import tpu as pltpu

D_MODEL = 1024
D_FF = 2816
A_WIDTH = 512
A_GROUPS = 8
A_GROUP_DIM = A_WIDTH // A_GROUPS
CHUNK = 128
B_HEADS = 8
B_QK_DIM = 64
B_V_DIM = 2 * B_QK_DIM
B_WIDTH = B_HEADS * B_V_DIM
ROPE_THETA = 500000.0
ROT_DIM = B_QK_DIM // 4
NORM_EPS = 1e-6
LN_EPS = 1e-5

LANES = 128
MXU_DIM = 256
VMEM_LIMIT = 56 * 1024 * 1024

FFN_TM = 512
PROJ_TM = 512
MIX_TM = 512
ATT_TQ = 1024
ATT_CB = MXU_DIM
ATT_LOOKAHEAD = 4
ATT_ONES_ROWS = 16
LOG2E = math.log2(math.e)

BF16 = jnp.bfloat16
F32 = jnp.float32


def _dot(a, b):
    return jnp.dot(a, b, preferred_element_type=F32)


def _rms_rows(x, gain, eps):
    ms = jnp.mean(x * x, axis=-1, keepdims=True)
    return x * lax.rsqrt(ms + eps) * gain


def _whole_vmem():
    return pl.BlockSpec(memory_space=pltpu.VMEM)


def _params(n_axes):
    return pltpu.CompilerParams(
        dimension_semantics=("arbitrary",) * n_axes,
        vmem_limit_bytes=VMEM_LIMIT,
    )


def _ffn_kernel(x_ref, gain_ref, wg_ref, wu_ref, wd_ref, o_ref, a_ref):
    x = x_ref[...]
    h = _rms_rows(x, gain_ref[...], NORM_EPS).astype(BF16)
    g = _dot(h, wg_ref[...])
    u = _dot(h, wu_ref[...])
    a_ref[...] = (jax.nn.silu(g) * u).astype(BF16)
    y = _dot(a_ref[...], wd_ref[...])
    o_ref[...] = x + 0.5 * y


def _ffn(x, gain, wg, wu, wd):
    t = x.shape[0]
    row = pl.BlockSpec((FFN_TM, D_MODEL), lambda i: (i, 0))
    return pl.pallas_call(
        _ffn_kernel,
        grid=(t // FFN_TM,),
        in_specs=[row, _whole_vmem(), _whole_vmem(), _whole_vmem(), _whole_vmem()],
        out_specs=row,
        out_shape=jax.ShapeDtypeStruct((t, D_MODEL), F32),
        scratch_shapes=[pltpu.VMEM((FFN_TM, D_FF), BF16)],
        compiler_params=_params(1),
        name="ffn",
    )(x, gain, wg, wu, wd)


def _qk_prep(z, gmat_ref, gain, cos_t, sin_lo, sin_hi, scale):
    zz = (z * z).astype(BF16)
    ms = jnp.concatenate(
        [_dot(zz[:, c:c + MXU_DIM], gmat_ref[...]) for c in range(0, B_WIDTH, MXU_DIM)], axis=1)
    zn = z * lax.rsqrt(ms + NORM_EPS) * gain
    half = ROT_DIM // 2
    up = pltpu.roll(zn, B_WIDTH - half, 1)
    dn = pltpu.roll(zn, half, 1)
    out = zn * cos_t + up * sin_lo + dn * sin_hi
    if scale != 1.0:
        out = out * scale
    return out.astype(BF16)


def _proj_kernel(x_ref, pos_ref, gain_ref, win_ref, lng_ref, lnb_ref, ws_ref, bs_ref, pa_ref,
                 gmat_ref, qg_ref, kg_ref, freq_ref, slo_ref, shi_ref,
                 ma_ref, sgb_ref, q_ref, k_ref, v_ref):
    tm = x_ref.shape[0]
    h = _rms_rows(x_ref[...], gain_ref[...], NORM_EPS).astype(BF16)

    def zcols(i):
        return _dot(h, win_ref[:, i * D_MODEL:(i + 1) * D_MODEL])

    sgb_ref[...] = jax.nn.sigmoid(zcols(1)).astype(BF16)

    zuv = zcols(2)
    zuv = 0.5 * zuv * (1.0 + lax.erf(zuv * (2.0 ** -0.5)))
    u = zuv[:, :A_WIDTH]
    v = zuv[:, A_WIDTH:]
    mu = jnp.mean(v, axis=-1, keepdims=True)
    vc = v - mu
    var = jnp.mean(vc * vc, axis=-1, keepdims=True)
    vn = (vc * lax.rsqrt(var + LN_EPS) * lng_ref[...] + lnb_ref[...]).astype(BF16)

    lane = lax.broadcasted_iota(jnp.int32, (CHUNK, LANES), 1)
    lo_half = lane < A_GROUP_DIM
    row_t = lax.broadcasted_iota(jnp.int32, (CHUNK, 2 * CHUNK), 0)
    col_s = lax.broadcasted_iota(jnp.int32, (CHUNK, 2 * CHUNK), 1) % CHUNK
    causal = col_s <= row_t
    zero = jnp.zeros((), BF16)
    f_rows = []
    for c in range(tm // CHUNK):
        f_cols = []
        for j in range(A_WIDTH // LANES):
            vb = vn[c * CHUNK:(c + 1) * CHUNK, j * LANES:(j + 1) * LANES]
            rhs = jnp.concatenate([jnp.where(lo_half, vb, zero), jnp.where(lo_half, zero, vb)], axis=0)
            wpair = jnp.where(causal, ws_ref[j], zero)
            f_cols.append(_dot(wpair, rhs))
        f_rows.append(jnp.concatenate(f_cols, axis=1) + bs_ref[...])
    f = jnp.concatenate(f_rows, axis=0)
    ya = (u * f).astype(BF16)
    ma_ref[...] = (jax.nn.sigmoid(zcols(0)) * _dot(ya, pa_ref[...])).astype(BF16)

    ang = pos_ref[...].astype(F32) * freq_ref[...]
    cos1 = jnp.cos(ang)
    sin1 = jnp.sin(ang)
    reps = B_WIDTH // LANES
    cos_t = jnp.concatenate([cos1] * reps, axis=1)
    sin_lo = jnp.concatenate([sin1 * slo_ref[...]] * reps, axis=1)
    sin_hi = jnp.concatenate([sin1 * shi_ref[...]] * reps, axis=1)

    q_ref[...] = _qk_prep(zcols(3), gmat_ref, qg_ref[...], cos_t, sin_lo, sin_hi, B_QK_DIM ** -0.5 * LOG2E)
    k_ref[...] = _qk_prep(zcols(4), gmat_ref, kg_ref[...], cos_t, sin_lo, sin_hi, 1.0)
    v_ref[...] = zcols(5).astype(BF16)


def _proj(x1, pos, gain, win, lng, lnb, ws_pair, bs_full, pa, gmat, qg, kg, freq, slo, shi):
    t = x1.shape[0]
    row = pl.BlockSpec((PROJ_TM, D_MODEL), lambda i: (i, 0))
    out = jax.ShapeDtypeStruct((t, D_MODEL), BF16)
    return pl.pallas_call(
        _proj_kernel,
        grid=(t // PROJ_TM,),
        in_specs=[row, pl.BlockSpec((PROJ_TM, 1), lambda i: (i, 0))] + [_whole_vmem()] * 13,
        out_specs=[row] * 5,
        out_shape=[out] * 5,
        compiler_params=_params(1),
        name="proj",
    )(x1, pos, gain, win, lng, lnb, ws_pair, bs_full, pa, gmat, qg, kg, freq, slo, shi)


def _attn_kernel(lamp_ref, q_ref, k_ref, v_ref, subg_ref, o_ref, acc1, acc2, m1, m2, sbuf, *, lam_init):
    qi = pl.program_id(2)
    cb = ATT_CB
    ncol = ATT_TQ // cb
    qt = q_ref[0].T
    drow = lax.broadcasted_iota(jnp.int32, (LANES, ATT_TQ), 0)
    zero = jnp.zeros((), BF16)
    qt1 = jnp.where(drow < B_QK_DIM, qt, zero)
    qt2 = jnp.where(drow < B_QK_DIM, zero, qt)
    ones = jnp.ones((ATT_ONES_ROWS, cb), BF16)
    krow = lax.broadcasted_iota(jnp.int32, (cb, cb), 0)
    qcol = lax.broadcasted_iota(jnp.int32, (cb, cb), 1)
    diag = krow <= qcol

    for acc, m in ((acc1, m1), (acc2, m2)):
        acc[...] = jnp.zeros_like(acc)
        m[...] = jnp.full_like(m, -jnp.inf)

    maps = ((qt1, acc1, m1), (qt2, acc2, m2))

    def load_k(kblk):
        return k_ref[0, pl.ds(pl.multiple_of(kblk * cb, cb), cb), :]

    def load_vte(kblk):
        v = v_ref[0, pl.ds(pl.multiple_of(kblk * cb, cb), cb), :]
        return jnp.concatenate([v.T, ones], axis=0)

    def qk(kb, c, mi):
        return _dot(kb, maps[mi][0][:, c * cb:(c + 1) * cb])

    def lead_cell(t):
        return t // 2, t % 2

    def run_cells(cells, next_kblk):
        n = len(cells)
        pending = []
        if next_kblk is not None:
            nkb = load_k(next_kblk)
        for i, (kb, vte, c, mi, masked) in enumerate(cells):
            if i + ATT_LOOKAHEAD < n:
                ahead = cells[i + ATT_LOOKAHEAD]
                pending.append(qk(ahead[0], ahead[2], ahead[3]))
            s = sbuf[i] if i < ATT_LOOKAHEAD else pending.pop(0)
            carry_out = i + ATT_LOOKAHEAD >= n and next_kblk is not None
            if carry_out:
                nxt = qk(nkb, *lead_cell(i + ATT_LOOKAHEAD - n))
            _, acc, m = maps[mi]
            cs = slice(c * cb, (c + 1) * cb)
            if masked:
                s = jnp.where(diag, s, -jnp.inf)
            m_old = m[:, cs]
            m_new = jnp.maximum(m_old, jnp.max(s, axis=0, keepdims=True))
            alpha = jnp.exp2(m_old - m_new)
            p = jnp.exp2(s - m_new).astype(BF16)
            acc[:, cs] = alpha * acc[:, cs] + _dot(vte, p)
            m[:, cs] = m_new
            if carry_out:
                sbuf[i + ATT_LOOKAHEAD - n] = nxt

    def body(j, carry):
        cells = []
        for sb in range(ncol):
            kb, vte = load_k(j * ncol + sb), load_vte(j * ncol + sb)
            cells += [(kb, vte, c, mi, False) for c in range(ncol) for mi in range(2)]
        run_cells(cells, (j + 1) * ncol)
        return carry

    kb0 = load_k(0)
    for t in range(ATT_LOOKAHEAD):
        sbuf[t] = qk(kb0, *lead_cell(t))
    lax.fori_loop(0, qi, body, 0)
    cells = []
    for sb in range(ncol):
        kb, vte = load_k(qi * ncol + sb), load_vte(qi * ncol + sb)
        cells += [(kb, vte, c, mi, c == sb) for c in range(sb, ncol) for mi in range(2)]
    run_cells(cells, None)

    lp = lamp_ref[...]
    lam = (jnp.exp(jnp.sum(lp[0:1] * lp[1:2], axis=1, keepdims=True))
           - jnp.exp(jnp.sum(lp[2:3] * lp[3:4], axis=1, keepdims=True)) + lam_init)
    ot = (acc1[:B_V_DIM] / acc1[B_V_DIM:B_V_DIM + 1]) - lam * (acc2[:B_V_DIM] / acc2[B_V_DIM:B_V_DIM + 1])
    o = _rms_rows(ot.T, subg_ref[...], NORM_EPS) * (1.0 - lam_init)
    o_ref[0] = o.astype(BF16)


def _attn(lamp, q, k, v, subg, lam_init):
    b, s, _ = q.shape
    assert ATT_LOOKAHEAD <= 2 * (ATT_TQ // ATT_CB)
    qspec = pl.BlockSpec((1, ATT_TQ, LANES), lambda bi, hi, i: (bi, i, hi))
    kvspec = pl.BlockSpec((1, s, LANES), lambda bi, hi, i: (bi, 0, hi))
    acc = pltpu.VMEM((B_V_DIM + ATT_ONES_ROWS, ATT_TQ), F32)
    stat = pltpu.VMEM((1, ATT_TQ), F32)
    return pl.pallas_call(
        functools.partial(_attn_kernel, lam_init=lam_init),
        grid=(b, B_HEADS, s // ATT_TQ),
        in_specs=[_whole_vmem(), qspec, kvspec, kvspec, _whole_vmem()],
        out_specs=qspec,
        out_shape=jax.ShapeDtypeStruct((b, s, B_WIDTH), BF16),
        scratch_shapes=[acc, acc, stat, stat, pltpu.VMEM((ATT_LOOKAHEAD, ATT_CB, ATT_CB), F32)],
        compiler_params=_params(3),
        name="attn",
    )(lamp, q, k, v, subg)


def _mixout_kernel(x_ref, ma_ref, sgb_ref, yb_ref, pb_ref, wo_ref, o_ref):
    m = ma_ref[...].astype(F32) + sgb_ref[...].astype(F32) * _dot(yb_ref[...], pb_ref[...])
    o_ref[...] = x_ref[...] + _dot(m.astype(BF16), wo_ref[...])


def _mixout(x1, ma, sgb, yb, pb, wo):
    t = x1.shape[0]
    row = pl.BlockSpec((MIX_TM, D_MODEL), lambda i: (i, 0))
    return pl.pallas_call(
        _mixout_kernel,
        grid=(t // MIX_TM,),
        in_specs=[row, row, row, row, _whole_vmem(), _whole_vmem()],
        out_specs=row,
        out_shape=jax.ShapeDtypeStruct((t, D_MODEL), F32),
        compiler_params=_params(1),
        name="mixout",
    )(x1, ma, sgb, yb, pb, wo)


def _rope_lane_tables():
    lane = jnp.arange(LANES) % B_QK_DIM
    half = ROT_DIM // 2
    inv = ROPE_THETA ** (-jnp.arange(0, ROT_DIM, 2, dtype=F32) / ROT_DIM)
    freq = jnp.where(lane < ROT_DIM, inv[lane % half], 0.0).astype(F32)[None, :]
    slo = jnp.where(lane < half, -1.0, 0.0).astype(F32)[None, :]
    shi = jnp.where((lane >= half) & (lane < ROT_DIM), 1.0, 0.0).astype(F32)[None, :]
    return freq, slo, shi


def kernel(x, positions, ffn1_norm, ffn1_w_gate, ffn1_w_up, ffn1_w_down, mix_norm, w_in, a_ln_gain, a_ln_bias, a_w_s, a_b_s, a_w_proj, b_q_norm, b_k_norm, b_lambda_q1, b_lambda_k1, b_lambda_q2, b_lambda_k2, b_subln, b_w_proj, w_out, ffn2_norm, ffn2_w_gate, ffn2_w_up, ffn2_w_down):
    bsz, seq, _ = x.shape
    depth = w_in.shape[0]
    t = bsz * seq
    xt = x.reshape(t, D_MODEL)
    pos = positions.reshape(t, 1)
    freq, slo, shi = _rope_lane_tables()
    grp = jnp.arange(MXU_DIM) // B_QK_DIM
    gmat = jnp.where(grp[:, None] == grp[None, :], 1.0 / B_QK_DIM, 0.0).astype(BF16)

    for l in range(depth):
        lam_init = 0.8 - 0.6 * math.exp(-0.3 * l)
        row = lambda a: a[l][None, :].astype(F32)
        xt = _ffn(xt, row(ffn1_norm), ffn1_w_gate[l].astype(BF16), ffn1_w_up[l].astype(BF16),
                  ffn1_w_down[l].astype(BF16))
        ws_pair = a_w_s[l].reshape(A_GROUPS // 2, 2, CHUNK, CHUNK).transpose(0, 2, 1, 3)
        ws_pair = ws_pair.reshape(A_GROUPS // 2, CHUNK, 2 * CHUNK).astype(BF16)
        bs_full = jnp.repeat(a_b_s[l].T, A_GROUP_DIM, axis=1).astype(F32)
        tile = lambda a: jnp.tile(a[l].astype(F32), B_WIDTH // B_QK_DIM)[None, :]
        ma, sgb, q, k, v = _proj(
            xt, pos, row(mix_norm), w_in[l].astype(BF16), row(a_ln_gain), row(a_ln_bias), ws_pair, bs_full,
            a_w_proj[l].astype(BF16), gmat, tile(b_q_norm), tile(b_k_norm), freq, slo, shi)
        lamp = jnp.stack([b_lambda_q1[l], b_lambda_k1[l], b_lambda_q2[l], b_lambda_k2[l]]).astype(F32)
        shp = (bsz, seq, B_WIDTH)
        yb = _attn(lamp, q.reshape(shp), k.reshape(shp), v.reshape(shp), row(b_subln), lam_init)
        xt = _mixout(xt, ma, sgb, yb.reshape(t, B_WIDTH), b_w_proj[l].astype(BF16), w_out[l].astype(BF16))
        xt = _ffn(xt, row(ffn2_norm), ffn2_w_gate[l].astype(BF16), ffn2_w_up[l].astype(BF16),
                  ffn2_w_down[l].astype(BF16))
    return xt.reshape(bsz, seq, D_MODEL)
```

```python
import functools
import math

import jax
import jax.numpy as jnp
from jax import lax
from jax.experimental import pallas as pl
from jax.experimental.pallas import tpu as pltpu

D_MODEL = 1024
D_FF = 2816
A_WIDTH = 512
A_GROUPS = 8
A_GROUP_DIM = A_WIDTH // A_GROUPS
CHUNK = 128
B_HEADS = 8
B_QK_DIM = 64
B_V_DIM = 2 * B_QK_DIM
B_WIDTH = B_HEADS * B_V_DIM
ROPE_THETA = 500000.0
ROT_DIM = B_QK_DIM // 4
NORM_EPS = 1e-6
LN_EPS = 1e-5

LANES = 128
MXU_DIM = 256
VMEM_LIMIT = 56 * 1024 * 1024

FFN_TM = 512
PROJ_TM = 512
MIX_TM = 512
ATT_TQ = 1024
ATT_CB = MXU_DIM
ATT_LOOKAHEAD = 4
ATT_ONES_ROWS = 16
LOG2E = math.log2(math.e)

BF16 = jnp.bfloat16
F32 = jnp.float32


def _dot(a, b):
    return jnp.dot(a, b, preferred_element_type=F32)


def _rms_rows(x, gain, eps):
    ms = jnp.mean(x * x, axis=-1, keepdims=True)
    return x * lax.rsqrt(ms + eps) * gain


def _whole_vmem():
    return pl.BlockSpec(memory_space=pltpu.VMEM)


def _params(n_axes):
    return pltpu.CompilerParams(
        dimension_semantics=("arbitrary",) * n_axes,
        vmem_limit_bytes=VMEM_LIMIT,
    )


def _ffn_kernel(x_ref, gain_ref, wg_ref, wu_ref, wd_ref, o_ref, a_ref):
    x = x_ref[...]
    h = _rms_rows(x, gain_ref[...], NORM_EPS).astype(BF16)
    g = _dot(h, wg_ref[...])
    u = _dot(h, wu_ref[...])
    a_ref[...] = (jax.nn.silu(g) * u).astype(BF16)
    y = _dot(a_ref[...], wd_ref[...])
    o_ref[...] = x + 0.5 * y


def _ffn(x, gain, wg, wu, wd):
    t = x.shape[0]
    row = pl.BlockSpec((FFN_TM, D_MODEL), lambda i: (i, 0))
    return pl.pallas_call(
        _ffn_kernel,
        grid=(t // FFN_TM,),
        in_specs=[row, _whole_vmem(), _whole_vmem(), _whole_vmem(), _whole_vmem()],
        out_specs=row,
        out_shape=jax.ShapeDtypeStruct((t, D_MODEL), F32),
        scratch_shapes=[pltpu.VMEM((FFN_TM, D_FF), BF16)],
        compiler_params=_params(1),
        name="ffn",
    )(x, gain, wg, wu, wd)


def _group_mean_sq(z, gmat_ref):
    zz = (z * z).astype(BF16)
    return jnp.concatenate(
        [_dot(zz[:, c:c + MXU_DIM], gmat_ref[...]) for c in range(0, B_WIDTH, MXU_DIM)], axis=1)


def _norm_rope(z, ms, gain, cos_t, sin_lo, sin_hi):
    zn = z * lax.rsqrt(ms + NORM_EPS) * gain
    half = ROT_DIM // 2
    up = pltpu.roll(zn, B_WIDTH - half, 1)
    dn = pltpu.roll(zn, half, 1)
    return (zn * cos_t + up * sin_lo + dn * sin_hi).astype(BF16)


def _proj_kernel(x_ref, pos_ref, gain_ref, win_ref, lng_ref, lnb_ref, ws_ref, bs_ref, pa_ref,
                 gmat_ref, qg_ref, kg_ref, freq_ref, rope_ref,
                 ma_ref, sgb_ref, qt_ref, k_ref, vt_ref):
    tm = x_ref.shape[0]
    h = _rms_rows(x_ref[...], gain_ref[...], NORM_EPS).astype(BF16)

    def zcols(i):
        return _dot(h, win_ref[:, i * D_MODEL:(i + 1) * D_MODEL])

    zq = zcols(3)

    ang = freq_ref[...] * pos_ref[0].astype(F32)
    parts = []
    for tr in (jnp.cos(ang), jnp.sin(ang)):
        hi = tr.astype(BF16).astype(F32)
        parts += [hi, tr - hi]
    parts.append(jnp.ones((LANES - 4 * ROT_DIM // 2, tm), F32))
    trig = jnp.concatenate(parts, axis=0).astype(BF16)
    tab = lax.dot_general(trig, rope_ref[...], (((0,), (0,)), ((), ())), preferred_element_type=F32)

    zk = zcols(4)
    msq = _group_mean_sq(zq, gmat_ref)
    zv = zcols(5)
    msk = _group_mean_sq(zk, gmat_ref)
    zuv = zcols(2)

    reps = B_WIDTH // LANES
    cos_t = jnp.concatenate([tab[:, :LANES]] * reps, axis=1)
    sin_lo = jnp.concatenate([tab[:, LANES:2 * LANES]] * reps, axis=1)
    sin_hi = jnp.concatenate([tab[:, 2 * LANES:]] * reps, axis=1)

    qt_ref[0] = _norm_rope(zq, msq, qg_ref[...], cos_t, sin_lo, sin_hi).T

    zga = zcols(0)

    k_ref[...] = _norm_rope(zk, msk, kg_ref[...], cos_t, sin_lo, sin_hi)
    vt_ref[0] = zv.astype(BF16).T

    zuv = 0.5 * zuv * (1.0 + lax.erf(zuv * (2.0 ** -0.5)))
    u = zuv[:, :A_WIDTH]
    v = zuv[:, A_WIDTH:]
    mu = jnp.mean(v, axis=-1, keepdims=True)
    vc = v - mu
    var = jnp.mean(vc * vc, axis=-1, keepdims=True)
    vn = (vc * lax.rsqrt(var + LN_EPS) * lng_ref[...] + lnb_ref[...]).astype(BF16)

    zgb = zcols(1)

    lane = lax.broadcasted_iota(jnp.int32, (CHUNK, LANES), 1)
    lo_half = lane < A_GROUP_DIM
    row_t = lax.broadcasted_iota(jnp.int32, (CHUNK, 2 * CHUNK), 0)
    col_s = lax.broadcasted_iota(jnp.int32, (CHUNK, 2 * CHUNK), 1) % CHUNK
    causal = col_s <= row_t
    zero = jnp.zeros((), BF16)
    f_rows = []
    for c in range(tm // CHUNK):
        f_cols = []
        for j in range(A_WIDTH // LANES):
            vb = vn[c * CHUNK:(c + 1) * CHUNK, j * LANES:(j + 1) * LANES]
            rhs = jnp.concatenate([jnp.where(lo_half, vb, zero), jnp.where(lo_half, zero, vb)], axis=0)
            wpair = jnp.where(causal, ws_ref[j], zero)
            f_cols.append(_dot(wpair, rhs))
        f_rows.append(jnp.concatenate(f_cols, axis=1) + bs_ref[...])
    f = jnp.concatenate(f_rows, axis=0)

    ya = (u * f).astype(BF16)
    ma_ref[...] = (jax.nn.sigmoid(zga) * _dot(ya, pa_ref[...])).astype(BF16)
    sgb_ref[...] = jax.nn.sigmoid(zgb).astype(BF16)


def _proj(x1, pos, gain, win, lng, lnb, ws_pair, bs_full, pa, gmat, qg, kg, freq, rope, bsz):
    t = x1.shape[0]
    nsb = t // bsz // PROJ_TM
    row = pl.BlockSpec((PROJ_TM, D_MODEL), lambda i: (i, 0))
    col = pl.BlockSpec((1, B_WIDTH, PROJ_TM), lambda i: (i // nsb, 0, i % nsb))
    out = jax.ShapeDtypeStruct((t, D_MODEL), BF16)
    out_t = jax.ShapeDtypeStruct((bsz, B_WIDTH, t // bsz), BF16)
    return pl.pallas_call(
        _proj_kernel,
        grid=(t // PROJ_TM,),
        in_specs=[row, pl.BlockSpec((1, 1, PROJ_TM), lambda i: (i, 0, 0))] + [_whole_vmem()] * 12,
        out_specs=[row, row, col, row, col],
        out_shape=[out, out, out_t, out, out_t],
        compiler_params=_params(1),
        name="proj",
    )(x1, pos, gain, win, lng, lnb, ws_pair, bs_full, pa, gmat, qg, kg, freq, rope)


def _attn_kernel(lamp_ref, qt_ref, k_ref, vt_ref, subg_ref, ot_ref, acc1, acc2, m1, m2, sbuf, *, lam_init):
    qi = pl.program_id(2)
    cb = ATT_CB
    ncol = ATT_TQ // cb
    qt = qt_ref[0]
    drow = lax.broadcasted_iota(jnp.int32, (LANES, ATT_TQ), 0)
    zero = jnp.zeros((), BF16)
    qt1 = jnp.where(drow < B_QK_DIM, qt, zero)
    qt2 = jnp.where(drow < B_QK_DIM, zero, qt)
    ones = jnp.ones((ATT_ONES_ROWS, cb), BF16)
    krow = lax.broadcasted_iota(jnp.int32, (cb, cb), 0)
    qcol = lax.broadcasted_iota(jnp.int32, (cb, cb), 1)
    diag = krow <= qcol

    for acc, m in ((acc1, m1), (acc2, m2)):
        acc[...] = jnp.zeros_like(acc)
        m[...] = jnp.full_like(m, -jnp.inf)

    maps = ((qt1, acc1, m1), (qt2, acc2, m2))

    def load_k(kblk):
        return k_ref[0, pl.ds(pl.multiple_of(kblk * cb, cb), cb), :]

    def load_vte(kblk):
        vt = vt_ref[0, :, pl.ds(pl.multiple_of(kblk * cb, cb), cb)]
        return jnp.concatenate([vt, ones], axis=0)

    def qk(kb, c, mi):
        return _dot(kb, maps[mi][0][:, c * cb:(c + 1) * cb])

    def lead_cell(t):
        return t // 2, t % 2

    def run_cells(cells, next_kblk):
        n = len(cells)
        pending = []
        if next_kblk is not None:
            nkb = load_k(next_kblk)
        for i, (kb, vte, c, mi, masked) in enumerate(cells):
            if i + ATT_LOOKAHEAD < n:
                ahead = cells[i + ATT_LOOKAHEAD]
                pending.append(qk(ahead[0], ahead[2], ahead[3]))
            s = sbuf[i] if i < ATT_LOOKAHEAD else pending.pop(0)
            carry_out = i + ATT_LOOKAHEAD >= n and next_kblk is not None
            if carry_out:
                nxt = qk(nkb, *lead_cell(i + ATT_LOOKAHEAD - n))
            _, acc, m = maps[mi]
            cs = slice(c * cb, (c + 1) * cb)
            if masked:
                s = jnp.where(diag, s, -jnp.inf)
            m_old = m[:, cs]
            m_new = jnp.maximum(m_old, jnp.max(s, axis=0, keepdims=True))
            alpha = jnp.exp2(m_old - m_new)
            p = jnp.exp2(s - m_new).astype(BF16)
            acc[:, cs] = alpha * acc[:, cs] + _dot(vte, p)
            m[:, cs] = m_new
            if carry_out:
                sbuf[i + ATT_LOOKAHEAD - n] = nxt

    def body(j, carry):
        cells = []
        for sb in range(ncol):
            kb, vte = load_k(j * ncol + sb), load_vte(j * ncol + sb)
            cells += [(kb, vte, c, mi, False) for c in range(ncol) for mi in range(2)]
        run_cells(cells, (j + 1) * ncol)
        return carry

    kb0 = load_k(0)
    for t in range(ATT_LOOKAHEAD):
        sbuf[t] = qk(kb0, *lead_cell(t))
    lax.fori_loop(0, qi, body, 0)
    cells = []
    for sb in range(ncol):
        kb, vte = load_k(qi * ncol + sb), load_vte(qi * ncol + sb)
        cells += [(kb, vte, c, mi, c == sb) for c in range(sb, ncol) for mi in range(2)]
    run_cells(cells, None)

    lp = lamp_ref[...]
    lam = (jnp.exp(jnp.sum(lp[0:1] * lp[1:2], axis=1, keepdims=True))
           - jnp.exp(jnp.sum(lp[2:3] * lp[3:4], axis=1, keepdims=True)) + lam_init)
    ot = (acc1[:B_V_DIM] / acc1[B_V_DIM:B_V_DIM + 1]) - lam * (acc2[:B_V_DIM] / acc2[B_V_DIM:B_V_DIM + 1])
    ms = jnp.mean(ot * ot, axis=0, keepdims=True)
    gain = jnp.concatenate([subg_ref[...]] * (ATT_TQ // LANES), axis=1)
    ot_ref[0] = (ot * lax.rsqrt(ms + NORM_EPS) * gain * (1.0 - lam_init)).astype(BF16)


def _attn(lamp, qt, k, vt, subg, lam_init):
    b, s, _ = k.shape
    assert ATT_LOOKAHEAD <= 2 * (ATT_TQ // ATT_CB)
    qspec = pl.BlockSpec((1, LANES, ATT_TQ), lambda bi, hi, i: (bi, hi, i))
    kspec = pl.BlockSpec((1, s, LANES), lambda bi, hi, i: (bi, 0, hi))
    vspec = pl.BlockSpec((1, LANES, s), lambda bi, hi, i: (bi, hi, 0))
    acc = pltpu.VMEM((B_V_DIM + ATT_ONES_ROWS, ATT_TQ), F32)
    stat = pltpu.VMEM((1, ATT_TQ), F32)
    return pl.pallas_call(
        functools.partial(_attn_kernel, lam_init=lam_init),
        grid=(b, B_HEADS, s // ATT_TQ),
        in_specs=[_whole_vmem(), qspec, kspec, vspec, _whole_vmem()],
        out_specs=qspec,
        out_shape=jax.ShapeDtypeStruct((b, B_WIDTH, s), BF16),
        scratch_shapes=[acc, acc, stat, stat, pltpu.VMEM((ATT_LOOKAHEAD, ATT_CB, ATT_CB), F32)],
        compiler_params=_params(3),
        name="attn",
    )(lamp, qt, k, vt, subg)


def _mixout_kernel(x_ref, ma_ref, sgb_ref, ybt_ref, pb_ref, wo_ref, o_ref):
    m = ma_ref[...].astype(F32) + sgb_ref[...].astype(F32) * _dot(ybt_ref[0].T, pb_ref[...])
    o_ref[...] = x_ref[...] + _dot(m.astype(BF16), wo_ref[...])


def _mixout(x1, ma, sgb, ybt, pb, wo):
    t = x1.shape[0]
    nsb = ybt.shape[2] // MIX_TM
    row = pl.BlockSpec((MIX_TM, D_MODEL), lambda i: (i, 0))
    col = pl.BlockSpec((1, B_WIDTH, MIX_TM), lambda i: (i // nsb, 0, i % nsb))
    return pl.pallas_call(
        _mixout_kernel,
        grid=(t // MIX_TM,),
        in_specs=[row, row, row, col, _whole_vmem(), _whole_vmem()],
        out_specs=row,
        out_shape=jax.ShapeDtypeStruct((t, D_MODEL), F32),
        compiler_params=_params(1),
        name="mixout",
    )(x1, ma, sgb, ybt, pb, wo)


def _rope_tables():
    half = ROT_DIM // 2
    freq = (ROPE_THETA ** (-jnp.arange(0, ROT_DIM, 2, dtype=F32) / ROT_DIM))[:, None]
    lane = jnp.arange(LANES) % B_QK_DIM
    hit = (lane[None, :] % half) == jnp.arange(half)[:, None]
    cos_m = jnp.where(hit & (lane < ROT_DIM)[None, :], 1.0, 0.0)
    slo_m = jnp.where(hit & (lane < half)[None, :], -1.0, 0.0)
    shi_m = jnp.where(hit & ((lane >= half) & (lane < ROT_DIM))[None, :], 1.0, 0.0)
    zero = jnp.zeros_like(cos_m)
    one_row = jnp.zeros((LANES - 4 * half, LANES), F32).at[0].set(jnp.where(lane >= ROT_DIM, 1.0, 0.0))
    rows = [jnp.concatenate([cos_m, zero, zero], axis=1)] * 2 + [jnp.concatenate([zero, slo_m, shi_m], axis=1)] * 2
    rows.append(jnp.concatenate([one_row, jnp.zeros_like(one_row), jnp.zeros_like(one_row)], axis=1))
    return freq, jnp.concatenate(rows, axis=0).astype(BF16)


def kernel(x, positions, ffn1_norm, ffn1_w_gate, ffn1_w_up, ffn1_w_down, mix_norm, w_in, a_ln_gain, a_ln_bias, a_w_s, a_b_s, a_w_proj, b_q_norm, b_k_norm, b_lambda_q1, b_lambda_k1, b_lambda_q2, b_lambda_k2, b_subln, b_w_proj, w_out, ffn2_norm, ffn2_w_gate, ffn2_w_up, ffn2_w_down):
    bsz, seq, _ = x.shape
    depth = w_in.shape[0]
    t = bsz * seq
    xt = x.reshape(t, D_MODEL)
    pos = positions.reshape(t // PROJ_TM, 1, PROJ_TM)
    freq, rope = _rope_tables()
    grp = jnp.arange(MXU_DIM) // B_QK_DIM
    gmat = jnp.where(grp[:, None] == grp[None, :], 1.0 / B_QK_DIM, 0.0).astype(BF16)

    for l in range(depth):
        lam_init = 0.8 - 0.6 * math.exp(-0.3 * l)
        row = lambda a: a[l][None, :].astype(F32)
        xt = _ffn(xt, row(ffn1_norm), ffn1_w_gate[l].astype(BF16), ffn1_w_up[l].astype(BF16),
                  ffn1_w_down[l].astype(BF16))
        ws_pair = a_w_s[l].reshape(A_GROUPS // 2, 2, CHUNK, CHUNK).transpose(0, 2, 1, 3)
        ws_pair = ws_pair.reshape(A_GROUPS // 2, CHUNK, 2 * CHUNK).astype(BF16)
        bs_full = jnp.repeat(a_b_s[l].T, A_GROUP_DIM, axis=1).astype(F32)
        tile = lambda a, scale: jnp.tile(a[l].astype(F32) * scale, B_WIDTH // B_QK_DIM)[None, :]
        ma, sgb, qt, k, vt = _proj(
            xt, pos, row(mix_norm), w_in[l].astype(BF16), row(a_ln_gain), row(a_ln_bias), ws_pair, bs_full,
            a_w_proj[l].astype(BF16), gmat, tile(b_q_norm, B_QK_DIM ** -0.5 * LOG2E), tile(b_k_norm, 1.0), freq, rope, bsz)
        lamp = jnp.stack([b_lambda_q1[l], b_lambda_k1[l], b_lambda_q2[l], b_lambda_k2[l]]).astype(F32)
        subg = jnp.broadcast_to(b_subln[l].astype(F32)[:, None], (B_V_DIM, LANES))
        ybt = _attn(lamp, qt, k.reshape(bsz, seq, B_WIDTH), vt, subg, lam_init)
        xt = _mixout(xt, ma, sgb, ybt, b_w_proj[l].astype(BF16), w_out[l].astype(BF16))
        xt = _ffn(xt, row(ffn2_norm), ffn2_w_gate[l].astype(BF16), ffn2_w_up[l].astype(BF16),
                  ffn2_w_down[l].astype(BF16))
    return xt.reshape(bsz, seq, D_MODEL)
```

```python
import functools
import math

import jax
import jax.numpy as jnp
from jax import lax
from jax.experimental import pallas as pl
from jax.experimental.pallas import tpu as pltpu

D_MODEL = 1024
D_FF = 2816
A_WIDTH = 512
A_GROUPS = 8
A_GROUP_DIM = A_WIDTH // A_GROUPS
CHUNK = 128
B_HEADS = 8
B_QK_DIM = 64
B_V_DIM = 2 * B_QK_DIM
B_WIDTH = B_HEADS * B_V_DIM
ROPE_THETA = 500000.0
ROT_DIM = B_QK_DIM // 4
NORM_EPS = 1e-6
LN_EPS = 1e-5

LANES = 128
MXU_DIM = 256
VMEM_LIMIT = 56 * 1024 * 1024

FFN_TM = 512
PROJ_TM = 512
MIX_TM = 512
ATT_TQ = 1024
ATT_CB = MXU_DIM
ATT_LOOKAHEAD = 4
ATT_ONES_ROWS = 16
LOG2E = math.log2(math.e)

BF16 = jnp.bfloat16
F32 = jnp.float32


def _dot(a, b):
    return jnp.dot(a, b, preferred_element_type=F32)


def _rms_rows(x, gain, eps):
    ms = jnp.mean(x * x, axis=-1, keepdims=True)
    return x * lax.rsqrt(ms + eps) * gain


def _whole_vmem():
    return pl.BlockSpec(memory_space=pltpu.VMEM)


def _params(n_axes):
    return pltpu.CompilerParams(
        dimension_semantics=("arbitrary",) * n_axes,
        vmem_limit_bytes=VMEM_LIMIT,
    )


def _ffn_kernel(x_ref, gain_ref, wg_ref, wu_ref, wd_ref, o_ref, a_ref):
    x = x_ref[...]
    h = _rms_rows(x, gain_ref[...], NORM_EPS).astype(BF16)
    g = _dot(h, wg_ref[...])
    u = _dot(h, wu_ref[...])
    a_ref[...] = (jax.nn.silu(g) * u).astype(BF16)
    y = _dot(a_ref[...], wd_ref[...])
    o_ref[...] = x + 0.5 * y


def _ffn(x, gain, wg, wu, wd):
    t = x.shape[0]
    row = pl.BlockSpec((FFN_TM, D_MODEL), lambda i: (i, 0))
    return pl.pallas_call(
        _ffn_kernel,
        grid=(t // FFN_TM,),
        in_specs=[row, _whole_vmem(), _whole_vmem(), _whole_vmem(), _whole_vmem()],
        out_specs=row,
        out_shape=jax.ShapeDtypeStruct((t, D_MODEL), F32),
        scratch_shapes=[pltpu.VMEM((FFN_TM, D_FF), BF16)],
        compiler_params=_params(1),
        name="ffn",
    )(x, gain, wg, wu, wd)


def _group_mean_sq(z, gmat_ref):
    zz = (z * z).astype(BF16)
    return jnp.concatenate(
        [_dot(zz[:, c:c + MXU_DIM], gmat_ref[...]) for c in range(0, B_WIDTH, MXU_DIM)], axis=1)


def _norm_rope(z, ms, gain, cos_t, sin_lo, sin_hi):
    zn = z * lax.rsqrt(ms + NORM_EPS) * gain
    half = ROT_DIM // 2
    up = pltpu.roll(zn, B_WIDTH - half, 1)
    dn = pltpu.roll(zn, half, 1)
    return (zn * cos_t + up * sin_lo + dn * sin_hi).astype(BF16)


def _proj_kernel(x_ref, pos_ref, gain_ref, win_ref, lng_ref, lnb_ref, ws_ref, bs_ref, pa_ref,
                 gmat_ref, qg_ref, kg_ref, freq_ref, rope_ref,
                 ma_ref, sgb_ref, qt_ref, k_ref, vt_ref):
    tm = x_ref.shape[0]
    h = _rms_rows(x_ref[...], gain_ref[...], NORM_EPS).astype(BF16)

    def zcols(i):
        return _dot(h, win_ref[:, i * D_MODEL:(i + 1) * D_MODEL])

    zq = zcols(3)

    ang = freq_ref[...] * pos_ref[0].astype(F32)
    parts = []
    for tr in (jnp.cos(ang), jnp.sin(ang)):
        hi = tr.astype(BF16).astype(F32)
        parts += [hi, tr - hi]
    parts.append(jnp.ones((LANES - 4 * ROT_DIM // 2, tm), F32))
    trig = jnp.concatenate(parts, axis=0).astype(BF16)
    tab = lax.dot_general(trig, rope_ref[...], (((0,), (0,)), ((), ())), preferred_element_type=F32)

    zk = zcols(4)
    msq = _group_mean_sq(zq, gmat_ref)
    zv = zcols(5)
    msk = _group_mean_sq(zk, gmat_ref)
    zuv = zcols(2)

    reps = B_WIDTH // LANES
    cos_t = jnp.concatenate([tab[:, :LANES]] * reps, axis=1)
    sin_lo = jnp.concatenate([tab[:, LANES:2 * LANES]] * reps, axis=1)
    sin_hi = jnp.concatenate([tab[:, 2 * LANES:]] * reps, axis=1)

    qt_ref[0] = _norm_rope(zq, msq, qg_ref[...], cos_t, sin_lo, sin_hi).T

    zga = zcols(0)

    k_ref[...] = _norm_rope(zk, msk, kg_ref[...], cos_t, sin_lo, sin_hi)
    vt_ref[0] = zv.astype(BF16).T

    zuv = 0.5 * zuv * (1.0 + lax.erf(zuv * (2.0 ** -0.5)))
    u = zuv[:, :A_WIDTH]
    v = zuv[:, A_WIDTH:]
    mu = jnp.mean(v, axis=-1, keepdims=True)
    vc = v - mu
    var = jnp.mean(vc * vc, axis=-1, keepdims=True)
    vn = (vc * lax.rsqrt(var + LN_EPS) * lng_ref[...] + lnb_ref[...]).astype(BF16)

    zgb = zcols(1)

    lane = lax.broadcasted_iota(jnp.int32, (CHUNK, LANES), 1)
    lo_half = lane < A_GROUP_DIM
    row_t = lax.broadcasted_iota(jnp.int32, (CHUNK, 2 * CHUNK), 0)
    col_s = lax.broadcasted_iota(jnp.int32, (CHUNK, 2 * CHUNK), 1) % CHUNK
    causal = col_s <= row_t
    zero = jnp.zeros((), BF16)
    f_rows = []
    for c in range(tm // CHUNK):
        f_cols = []
        for j in range(A_WIDTH // LANES):
            vb = vn[c * CHUNK:(c + 1) * CHUNK, j * LANES:(j + 1) * LANES]
            rhs = jnp.concatenate([jnp.where(lo_half, vb, zero), jnp.where(lo_half, zero, vb)], axis=0)
            wpair = jnp.where(causal, ws_ref[j], zero)
            f_cols.append(_dot(wpair, rhs))
        f_rows.append(jnp.concatenate(f_cols, axis=1) + bs_ref[...])
    f = jnp.concatenate(f_rows, axis=0)

    ya = (u * f).astype(BF16)
    ma_ref[...] = (jax.nn.sigmoid(zga) * _dot(ya, pa_ref[...])).astype(BF16)
    sgb_ref[...] = jax.nn.sigmoid(zgb).astype(BF16)


def _proj(x1, pos, gain, win, lng, lnb, ws_pair, bs_full, pa, gmat, qg, kg, freq, rope, bsz):
    t = x1.shape[0]
    nsb = t // bsz // PROJ_TM
    row = pl.BlockSpec((PROJ_TM, D_MODEL), lambda i: (i, 0))
    col = pl.BlockSpec((1, B_WIDTH, PROJ_TM), lambda i: (i // nsb, 0, i % nsb))
    out = jax.ShapeDtypeStruct((t, D_MODEL), BF16)
    out_t = jax.ShapeDtypeStruct((bsz, B_WIDTH, t // bsz), BF16)
    return pl.pallas_call(
        _proj_kernel,
        grid=(t // PROJ_TM,),
        in_specs=[row, pl.BlockSpec((1, 1, PROJ_TM), lambda i: (i, 0, 0))] + [_whole_vmem()] * 12,
        out_specs=[row, row, col, row, col],
        out_shape=[out, out, out_t, out, out_t],
        compiler_params=_params(1),
        name="proj",
    )(x1, pos, gain, win, lng, lnb, ws_pair, bs_full, pa, gmat, qg, kg, freq, rope)


def _attn_kernel(lamp_ref, qt_ref, k_ref, vt_ref, subg_ref, ot_ref, acc1, acc2, m1, m2, sbuf, *, lam_init):
    qi = pl.program_id(2)
    cb = ATT_CB
    ncol = ATT_TQ // cb
    qt = qt_ref[0]
    drow = lax.broadcasted_iota(jnp.int32, (LANES, ATT_TQ), 0)
    zero = jnp.zeros((), BF16)
    qt1 = jnp.where(drow < B_QK_DIM, qt, zero)
    qt2 = jnp.where(drow < B_QK_DIM, zero, qt)
    ones = jnp.ones((ATT_ONES_ROWS, cb), BF16)
    krow = lax.broadcasted_iota(jnp.int32, (cb, cb), 0)
    qcol = lax.broadcasted_iota(jnp.int32, (cb, cb), 1)
    diag = krow <= qcol

    for acc, m in ((acc1, m1), (acc2, m2)):
        acc[...] = jnp.zeros_like(acc)
        m[...] = jnp.full_like(m, -jnp.inf)

    maps = ((qt1, acc1, m1), (qt2, acc2, m2))

    def load_k(kblk):
        return k_ref[0, pl.ds(pl.multiple_of(kblk * cb, cb), cb), :]

    def load_vte(kblk):
        vt = vt_ref[0, :, pl.ds(pl.multiple_of(kblk * cb, cb), cb)]
        return jnp.concatenate([vt, ones], axis=0)

    def qk(kb, c, mi):
        return _dot(kb, maps[mi][0][:, c * cb:(c + 1) * cb])

    def lead_cell(t):
        return t // 2, t % 2

    def run_cells(cells, next_kblk):
        n = len(cells)
        pending = []
        if next_kblk is not None:
            nkb = load_k(next_kblk)
        for i, (kb, vte, c, mi, masked) in enumerate(cells):
            if i + ATT_LOOKAHEAD < n:
                ahead = cells[i + ATT_LOOKAHEAD]
                pending.append(qk(ahead[0], ahead[2], ahead[3]))
            s = sbuf[i] if i < ATT_LOOKAHEAD else pending.pop(0)
            carry_out = i + ATT_LOOKAHEAD >= n and next_kblk is not None
            if carry_out:
                nxt = qk(nkb, *lead_cell(i + ATT_LOOKAHEAD - n))
            _, acc, m = maps[mi]
            cs = slice(c * cb, (c + 1) * cb)
            if masked:
                s = jnp.where(diag, s, -jnp.inf)
            m_old = m[:, cs]
            m_new = jnp.maximum(m_old, jnp.max(s, axis=0, keepdims=True))
            alpha = jnp.exp2(m_old - m_new)
            p = jnp.exp2(s - m_new).astype(BF16)
            acc[:, cs] = alpha * acc[:, cs] + _dot(vte, p)
            m[:, cs] = m_new
            if carry_out:
                sbuf[i + ATT_LOOKAHEAD - n] = nxt

    def full_blocks(first_kblk, nblk):
        cells = []
        for sb in range(nblk):
            kb, vte = load_k(first_kblk + sb), load_vte(first_kblk + sb)
            cells += [(kb, vte, c, mi, False) for c in range(ncol) for mi in range(2)]
        run_cells(cells, first_kblk + nblk)

    def body_pair(j, carry):
        full_blocks(j * 2 * ncol, 2 * ncol)
        return carry

    def body_single(j, carry):
        full_blocks((qi - 1) * ncol, ncol)
        return carry

    kb0 = load_k(0)
    for t in range(ATT_LOOKAHEAD):
        sbuf[t] = qk(kb0, *lead_cell(t))
    lax.fori_loop(0, qi // 2, body_pair, 0)
    lax.fori_loop(0, qi % 2, body_single, 0)
    cells = []
    for sb in range(ncol):
        kb, vte = load_k(qi * ncol + sb), load_vte(qi * ncol + sb)
        cells += [(kb, vte, c, mi, c == sb) for c in range(sb, ncol) for mi in range(2)]
    run_cells(cells, None)

    lp = lamp_ref[...]
    lam = (jnp.exp(jnp.sum(lp[0:1] * lp[1:2], axis=1, keepdims=True))
           - jnp.exp(jnp.sum(lp[2:3] * lp[3:4], axis=1, keepdims=True)) + lam_init)
    for c in range(ncol):
        cs = slice(c * cb, (c + 1) * cb)
        r1 = 1.0 / acc1[B_V_DIM:B_V_DIM + 1, cs]
        r2 = lam / acc2[B_V_DIM:B_V_DIM + 1, cs]
        ot = acc1[:B_V_DIM, cs] * r1 - acc2[:B_V_DIM, cs] * r2
        ms = jnp.mean(ot * ot, axis=0, keepdims=True)
        gain = jnp.concatenate([subg_ref[...]] * (cb // LANES), axis=1)
        ot_ref[0, :, cs] = (ot * (lax.rsqrt(ms + NORM_EPS) * (1.0 - lam_init)) * gain).astype(BF16)


def _attn(lamp, qt, k, vt, subg, lam_init):
    b, s, _ = k.shape
    assert ATT_LOOKAHEAD <= 2 * (ATT_TQ // ATT_CB)
    qspec = pl.BlockSpec((1, LANES, ATT_TQ), lambda bi, hi, i: (bi, hi, i))
    kspec = pl.BlockSpec((1, s, LANES), lambda bi, hi, i: (bi, 0, hi))
    vspec = pl.BlockSpec((1, LANES, s), lambda bi, hi, i: (bi, hi, 0))
    acc = pltpu.VMEM((B_V_DIM + ATT_ONES_ROWS, ATT_TQ), F32)
    stat = pltpu.VMEM((1, ATT_TQ), F32)
    return pl.pallas_call(
        functools.partial(_attn_kernel, lam_init=lam_init),
        grid=(b, B_HEADS, s // ATT_TQ),
        in_specs=[_whole_vmem(), qspec, kspec, vspec, _whole_vmem()],
        out_specs=qspec,
        out_shape=jax.ShapeDtypeStruct((b, B_WIDTH, s), BF16),
        scratch_shapes=[acc, acc, stat, stat, pltpu.VMEM((ATT_LOOKAHEAD, ATT_CB, ATT_CB), F32)],
        compiler_params=_params(3),
        name="attn",
    )(lamp, qt, k, vt, subg)


def _mixffn_kernel(x_ref, ma_ref, sgb_ref, ybt_ref, pb_ref, wo_ref, gain_ref, wg_ref, wu_ref, wd_ref, o_ref, a_ref):
    m = ma_ref[...].astype(F32) + sgb_ref[...].astype(F32) * _dot(ybt_ref[0].T, pb_ref[...])
    x = x_ref[...] + _dot(m.astype(BF16), wo_ref[...])
    h = _rms_rows(x, gain_ref[...], NORM_EPS).astype(BF16)
    g = _dot(h, wg_ref[...])
    u = _dot(h, wu_ref[...])
    a_ref[...] = (jax.nn.silu(g) * u).astype(BF16)
    y = _dot(a_ref[...], wd_ref[...])
    o_ref[...] = x + 0.5 * y


def _mixffn(x1, ma, sgb, ybt, pb, wo, gain, wg, wu, wd):
    t = x1.shape[0]
    nsb = ybt.shape[2] // MIX_TM
    row = pl.BlockSpec((MIX_TM, D_MODEL), lambda i: (i, 0))
    col = pl.BlockSpec((1, B_WIDTH, MIX_TM), lambda i: (i // nsb, 0, i % nsb))
    return pl.pallas_call(
        _mixffn_kernel,
        grid=(t // MIX_TM,),
        in_specs=[row, row, row, col] + [_whole_vmem()] * 6,
        out_specs=row,
        out_shape=jax.ShapeDtypeStruct((t, D_MODEL), F32),
        scratch_shapes=[pltpu.VMEM((MIX_TM, D_FF), BF16)],
        compiler_params=_params(1),
        name="mixffn",
    )(x1, ma, sgb, ybt, pb, wo, gain, wg, wu, wd)


def _rope_tables():
    half = ROT_DIM // 2
    freq = (ROPE_THETA ** (-jnp.arange(0, ROT_DIM, 2, dtype=F32) / ROT_DIM))[:, None]
    lane = jnp.arange(LANES) % B_QK_DIM
    hit = (lane[None, :] % half) == jnp.arange(half)[:, None]
    cos_m = jnp.where(hit & (lane < ROT_DIM)[None, :], 1.0, 0.0)
    slo_m = jnp.where(hit & (lane < half)[None, :], -1.0, 0.0)
    shi_m = jnp.where(hit & ((lane >= half) & (lane < ROT_DIM))[None, :], 1.0, 0.0)
    zero = jnp.zeros_like(cos_m)
    one_row = jnp.zeros((LANES - 4 * half, LANES), F32).at[0].set(jnp.where(lane >= ROT_DIM, 1.0, 0.0))
    rows = [jnp.concatenate([cos_m, zero, zero], axis=1)] * 2 + [jnp.concatenate([zero, slo_m, shi_m], axis=1)] * 2
    rows.append(jnp.concatenate([one_row, jnp.zeros_like(one_row), jnp.zeros_like(one_row)], axis=1))
    return freq, jnp.concatenate(rows, axis=0).astype(BF16)


def kernel(x, positions, ffn1_norm, ffn1_w_gate, ffn1_w_up, ffn1_w_down, mix_norm, w_in, a_ln_gain, a_ln_bias, a_w_s, a_b_s, a_w_proj, b_q_norm, b_k_norm, b_lambda_q1, b_lambda_k1, b_lambda_q2, b_lambda_k2, b_subln, b_w_proj, w_out, ffn2_norm, ffn2_w_gate, ffn2_w_up, ffn2_w_down):
    bsz, seq, _ = x.shape
    depth = w_in.shape[0]
    t = bsz * seq
    xt = x.reshape(t, D_MODEL)
    pos = positions.reshape(t // PROJ_TM, 1, PROJ_TM)
    freq, rope = _rope_tables()
    grp = jnp.arange(MXU_DIM) // B_QK_DIM
    gmat = jnp.where(grp[:, None] == grp[None, :], 1.0 / B_QK_DIM, 0.0).astype(BF16)

    for l in range(depth):
        lam_init = 0.8 - 0.6 * math.exp(-0.3 * l)
        row = lambda a: a[l][None, :].astype(F32)
        xt = _ffn(xt, row(ffn1_norm), ffn1_w_gate[l].astype(BF16), ffn1_w_up[l].astype(BF16),
                  ffn1_w_down[l].astype(BF16))
        ws_pair = a_w_s[l].reshape(A_GROUPS // 2, 2, CHUNK, CHUNK).transpose(0, 2, 1, 3)
        ws_pair = ws_pair.reshape(A_GROUPS // 2, CHUNK, 2 * CHUNK).astype(BF16)
        bs_full = jnp.repeat(a_b_s[l].T, A_GROUP_DIM, axis=1).astype(F32)
        tile = lambda a, scale: jnp.tile(a[l].astype(F32) * scale, B_WIDTH // B_QK_DIM)[None, :]
        ma, sgb, qt, k, vt = _proj(
            xt, pos, row(mix_norm), w_in[l].astype(BF16), row(a_ln_gain), row(a_ln_bias), ws_pair, bs_full,
            a_w_proj[l].astype(BF16), gmat, tile(b_q_norm, B_QK_DIM ** -0.5 * LOG2E), tile(b_k_norm, 1.0), freq, rope, bsz)
        lamp = jnp.stack([b_lambda_q1[l], b_lambda_k1[l], b_lambda_q2[l], b_lambda_k2[l]]).astype(F32)
        subg = jnp.broadcast_to(b_subln[l].astype(F32)[:, None], (B_V_DIM, LANES))
        ybt = _attn(lamp, qt, k.reshape(bsz, seq, B_WIDTH), vt, subg, lam_init)
        xt = _mixffn(xt, ma, sgb, ybt, b_w_proj[l].astype(BF16), w_out[l].astype(BF16), row(ffn2_norm),
                     ffn2_w_gate[l].astype(BF16), ffn2_w_up[l].astype(BF16), ffn2_w_down[l].astype(BF16))
    return xt.reshape(bsz, seq, D_MODEL)
```

```python
import functools
import math

import jax
import jax.numpy as jnp
from jax import lax
from jax.experimental import pallas as pl
from jax.experimental.pallas import tpu as pltpu

D_MODEL = 1024
D_FF = 2816
A_WIDTH = 512
A_GROUPS = 8
A_GROUP_DIM = A_WIDTH // A_GROUPS
CHUNK = 128
B_HEADS = 8
B_QK_DIM = 64
B_V_DIM = 2 * B_QK_DIM
B_WIDTH = B_HEADS * B_V_DIM
ROPE_THETA = 500000.0
ROT_DIM = B_QK_DIM // 4
NORM_EPS = 1e-6
LN_EPS = 1e-5

LANES = 128
MXU_DIM = 256
VMEM_LIMIT = 56 * 1024 * 1024

FFN_TM = 512
PROJ_TM = 512
MIX_TM = 512
ATT_CB = MXU_DIM
ATT_LOOKAHEAD = 4
ATT_ONES_ROWS = 16
LOG2E = math.log2(math.e)

BF16 = jnp.bfloat16
F32 = jnp.float32


def _dot(a, b):
    return jnp.dot(a, b, preferred_element_type=F32)


def _rms_rows(x, gain, eps):
    ms = jnp.mean(x * x, axis=-1, keepdims=True)
    return x * lax.rsqrt(ms + eps) * gain


def _whole_vmem():
    return pl.BlockSpec(memory_space=pltpu.VMEM)


def _params(n_axes):
    return pltpu.CompilerParams(
        dimension_semantics=("arbitrary",) * n_axes,
        vmem_limit_bytes=VMEM_LIMIT,
    )


def _ffn_kernel(x_ref, gain_ref, wg_ref, wu_ref, wd_ref, o_ref, a_ref):
    x = x_ref[...]
    h = _rms_rows(x, gain_ref[...], NORM_EPS).astype(BF16)
    g = _dot(h, wg_ref[...])
    u = _dot(h, wu_ref[...])
    a_ref[...] = (jax.nn.silu(g) * u).astype(BF16)
    y = _dot(a_ref[...], wd_ref[...])
    o_ref[...] = x + 0.5 * y


def _ffn(x, gain, wg, wu, wd):
    t = x.shape[0]
    row = pl.BlockSpec((FFN_TM, D_MODEL), lambda i: (i, 0))
    return pl.pallas_call(
        _ffn_kernel,
        grid=(t // FFN_TM,),
        in_specs=[row, _whole_vmem(), _whole_vmem(), _whole_vmem(), _whole_vmem()],
        out_specs=row,
        out_shape=jax.ShapeDtypeStruct((t, D_MODEL), F32),
        scratch_shapes=[pltpu.VMEM((FFN_TM, D_FF), BF16)],
        compiler_params=_params(1),
        name="ffn",
    )(x, gain, wg, wu, wd)


def _group_mean_sq(z, gmat_ref):
    zz = (z * z).astype(BF16)
    return jnp.concatenate(
        [_dot(zz[:, c:c + MXU_DIM], gmat_ref[...]) for c in range(0, B_WIDTH, MXU_DIM)], axis=1)


def _norm_rope(z, ms, gain, cos_t, sin_lo, sin_hi):
    zn = z * lax.rsqrt(ms + NORM_EPS) * gain
    half = ROT_DIM // 2
    up = pltpu.roll(zn, B_WIDTH - half, 1)
    dn = pltpu.roll(zn, half, 1)
    return (zn * cos_t + up * sin_lo + dn * sin_hi).astype(BF16)


def _proj_kernel(x_ref, pos_ref, gain_ref, win_ref, lng_ref, lnb_ref, ws_ref, bs_ref, pa_ref,
                 gmat_ref, qg_ref, kg_ref, freq_ref, rope_ref,
                 ma_ref, sgb_ref, qt_ref, k_ref, vt_ref):
    tm = x_ref.shape[0]
    h = _rms_rows(x_ref[...], gain_ref[...], NORM_EPS).astype(BF16)

    def zcols(i):
        return _dot(h, win_ref[:, i * D_MODEL:(i + 1) * D_MODEL])

    zq = zcols(3)

    ang = freq_ref[...] * pos_ref[0].astype(F32)
    parts = []
    for tr in (jnp.cos(ang), jnp.sin(ang)):
        hi = tr.astype(BF16).astype(F32)
        parts += [hi, tr - hi]
    parts.append(jnp.ones((LANES - 4 * ROT_DIM // 2, tm), F32))
    trig = jnp.concatenate(parts, axis=0).astype(BF16)
    tab = lax.dot_general(trig, rope_ref[...], (((0,), (0,)), ((), ())), preferred_element_type=F32)

    zk = zcols(4)
    msq = _group_mean_sq(zq, gmat_ref)
    zv = zcols(5)
    msk = _group_mean_sq(zk, gmat_ref)
    zuv = zcols(2)

    reps = B_WIDTH // LANES
    cos_t = jnp.concatenate([tab[:, :LANES]] * reps, axis=1)
    sin_lo = jnp.concatenate([tab[:, LANES:2 * LANES]] * reps, axis=1)
    sin_hi = jnp.concatenate([tab[:, 2 * LANES:]] * reps, axis=1)

    qt_ref[0] = _norm_rope(zq, msq, qg_ref[...], cos_t, sin_lo, sin_hi).T

    zga = zcols(0)

    k_ref[...] = _norm_rope(zk, msk, kg_ref[...], cos_t, sin_lo, sin_hi)
    vt_ref[0] = zv.astype(BF16).T

    zuv = 0.5 * zuv * (1.0 + lax.erf(zuv * (2.0 ** -0.5)))
    u = zuv[:, :A_WIDTH]
    v = zuv[:, A_WIDTH:]
    mu = jnp.mean(v, axis=-1, keepdims=True)
    vc = v - mu
    var = jnp.mean(vc * vc, axis=-1, keepdims=True)
    vn = (vc * lax.rsqrt(var + LN_EPS) * lng_ref[...] + lnb_ref[...]).astype(BF16)

    zgb = zcols(1)

    lane = lax.broadcasted_iota(jnp.int32, (CHUNK, LANES), 1)
    lo_half = lane < A_GROUP_DIM
    row_t = lax.broadcasted_iota(jnp.int32, (CHUNK, 2 * CHUNK), 0)
    col_s = lax.broadcasted_iota(jnp.int32, (CHUNK, 2 * CHUNK), 1) % CHUNK
    causal = col_s <= row_t
    zero = jnp.zeros((), BF16)
    f_rows = []
    for c in range(tm // CHUNK):
        f_cols = []
        for j in range(A_WIDTH // LANES):
            vb = vn[c * CHUNK:(c + 1) * CHUNK, j * LANES:(j + 1) * LANES]
            rhs = jnp.concatenate([jnp.where(lo_half, vb, zero), jnp.where(lo_half, zero, vb)], axis=0)
            wpair = jnp.where(causal, ws_ref[j], zero)
            f_cols.append(_dot(wpair, rhs))
        f_rows.append(jnp.concatenate(f_cols, axis=1) + bs_ref[...])
    f = jnp.concatenate(f_rows, axis=0)

    ya = (u * f).astype(BF16)
    ma_ref[...] = (jax.nn.sigmoid(zga) * _dot(ya, pa_ref[...])).astype(BF16)
    sgb_ref[...] = jax.nn.sigmoid(zgb).astype(BF16)


def _proj(x1, pos, gain, win, lng, lnb, ws_pair, bs_full, pa, gmat, qg, kg, freq, rope, bsz):
    t = x1.shape[0]
    nsb = t // bsz // PROJ_TM
    row = pl.BlockSpec((PROJ_TM, D_MODEL), lambda i: (i, 0))
    col = pl.BlockSpec((1, B_WIDTH, PROJ_TM), lambda i: (i // nsb, 0, i % nsb))
    out = jax.ShapeDtypeStruct((t, D_MODEL), BF16)
    out_t = jax.ShapeDtypeStruct((bsz, B_WIDTH, t // bsz), BF16)
    return pl.pallas_call(
        _proj_kernel,
        grid=(t // PROJ_TM,),
        in_specs=[row, pl.BlockSpec((1, 1, PROJ_TM), lambda i: (i, 0, 0))] + [_whole_vmem()] * 12,
        out_specs=[row, row, col, row, col],
        out_shape=[out, out, out_t, out, out_t],
        compiler_params=_params(1),
        name="proj",
    )(x1, pos, gain, win, lng, lnb, ws_pair, bs_full, pa, gmat, qg, kg, freq, rope)


def _attn_kernel(lamp_ref, qt_ref, k_ref, vt_ref, subg_ref, ot_ref, acc1, acc2, m1, m2, *, lam_init):
    cb = ATT_CB
    seq = qt_ref.shape[2]
    nblk = seq // cb
    drow = lax.broadcasted_iota(jnp.int32, (LANES, cb), 0)
    lo_rows = drow < B_QK_DIM
    zero = jnp.zeros((), BF16)
    ones = jnp.ones((ATT_ONES_ROWS, cb), BF16)
    krow = lax.broadcasted_iota(jnp.int32, (cb, cb), 0)
    qcol = lax.broadcasted_iota(jnp.int32, (cb, cb), 1)
    diag = krow <= qcol
    stats = ((acc1, m1), (acc2, m2))

    def score(kblk, c, mi):
        qt = qt_ref[0, :, c * cb:(c + 1) * cb]
        qz = jnp.where(lo_rows, qt, zero) if mi == 0 else jnp.where(lo_rows, zero, qt)
        return _dot(k_ref[0, kblk * cb:(kblk + 1) * cb, :], qz)

    cells = [(kblk, c, mi) for kblk in range(nblk) for c in range(kblk, nblk) for mi in range(2)]
    pending = [score(*cell) for cell in cells[:ATT_LOOKAHEAD]]
    vte = None
    for i, (kblk, c, mi) in enumerate(cells):
        if i + ATT_LOOKAHEAD < len(cells):
            pending.append(score(*cells[i + ATT_LOOKAHEAD]))
        s = pending.pop(0)
        if c == kblk:
            s = jnp.where(diag, s, -jnp.inf)
            if mi == 0:
                vte = jnp.concatenate([vt_ref[0, :, kblk * cb:(kblk + 1) * cb], ones], axis=0)
        acc, m = stats[mi]
        cs = slice(c * cb, (c + 1) * cb)
        bmax = jnp.max(s, axis=0, keepdims=True)
        if kblk == 0:
            m_new = bmax
            acc[:, cs] = _dot(vte, jnp.exp2(s - m_new).astype(BF16))
        else:
            m_old = m[:, cs]
            m_new = jnp.maximum(m_old, bmax)
            alpha = jnp.exp2(m_old - m_new)
            acc[:, cs] = alpha * acc[:, cs] + _dot(vte, jnp.exp2(s - m_new).astype(BF16))
        m[:, cs] = m_new

    lp = lamp_ref[...]
    lam = (jnp.exp(jnp.sum(lp[0:1] * lp[1:2], axis=1, keepdims=True))
           - jnp.exp(jnp.sum(lp[2:3] * lp[3:4], axis=1, keepdims=True)) + lam_init)
    for c in range(nblk):
        cs = slice(c * cb, (c + 1) * cb)
        r1 = 1.0 / acc1[B_V_DIM:B_V_DIM + 1, cs]
        r2 = lam / acc2[B_V_DIM:B_V_DIM + 1, cs]
        ot = acc1[:B_V_DIM, cs] * r1 - acc2[:B_V_DIM, cs] * r2
        ms = jnp.mean(ot * ot, axis=0, keepdims=True)
        gain = jnp.concatenate([subg_ref[...]] * (cb // LANES), axis=1)
        ot_ref[0, :, cs] = (ot * (lax.rsqrt(ms + NORM_EPS) * (1.0 - lam_init)) * gain).astype(BF16)


def _attn(lamp, qt, k, vt, subg, lam_init):
    b, s, _ = k.shape
    tspec = pl.BlockSpec((1, LANES, s), lambda bi, hi: (bi, hi, 0))
    kspec = pl.BlockSpec((1, s, LANES), lambda bi, hi: (bi, 0, hi))
    acc = pltpu.VMEM((B_V_DIM + ATT_ONES_ROWS, s), F32)
    stat = pltpu.VMEM((1, s), F32)
    return pl.pallas_call(
        functools.partial(_attn_kernel, lam_init=lam_init),
        grid=(b, B_HEADS),
        in_specs=[_whole_vmem(), tspec, kspec, tspec, _whole_vmem()],
        out_specs=tspec,
        out_shape=jax.ShapeDtypeStruct((b, B_WIDTH, s), BF16),
        scratch_shapes=[acc, acc, stat, stat],
        compiler_params=_params(2),
        name="attn",
    )(lamp, qt, k, vt, subg)


def _mixffn_kernel(x_ref, ma_ref, sgb_ref, ybt_ref, pb_ref, wo_ref, gain_ref, wg_ref, wu_ref, wd_ref, o_ref, a_ref):
    m = ma_ref[...].astype(F32) + sgb_ref[...].astype(F32) * _dot(ybt_ref[0].T, pb_ref[...])
    x = x_ref[...] + _dot(m.astype(BF16), wo_ref[...])
    h = _rms_rows(x, gain_ref[...], NORM_EPS).astype(BF16)
    g = _dot(h, wg_ref[...])
    u = _dot(h, wu_ref[...])
    a_ref[...] = (jax.nn.silu(g) * u).astype(BF16)
    y = _dot(a_ref[...], wd_ref[...])
    o_ref[...] = x + 0.5 * y


def _mixffn(x1, ma, sgb, ybt, pb, wo, gain, wg, wu, wd):
    t = x1.shape[0]
    nsb = ybt.shape[2] // MIX_TM
    row = pl.BlockSpec((MIX_TM, D_MODEL), lambda i: (i, 0))
    col = pl.BlockSpec((1, B_WIDTH, MIX_TM), lambda i: (i // nsb, 0, i % nsb))
    return pl.pallas_call(
        _mixffn_kernel,
        grid=(t // MIX_TM,),
        in_specs=[row, row, row, col] + [_whole_vmem()] * 6,
        out_specs=row,
        out_shape=jax.ShapeDtypeStruct((t, D_MODEL), F32),
        scratch_shapes=[pltpu.VMEM((MIX_TM, D_FF), BF16)],
        compiler_params=_params(1),
        name="mixffn",
    )(x1, ma, sgb, ybt, pb, wo, gain, wg, wu, wd)


def _rope_tables():
    half = ROT_DIM // 2
    freq = (ROPE_THETA ** (-jnp.arange(0, ROT_DIM, 2, dtype=F32) / ROT_DIM))[:, None]
    lane = jnp.arange(LANES) % B_QK_DIM
    hit = (lane[None, :] % half) == jnp.arange(half)[:, None]
    cos_m = jnp.where(hit & (lane < ROT_DIM)[None, :], 1.0, 0.0)
    slo_m = jnp.where(hit & (lane < half)[None, :], -1.0, 0.0)
    shi_m = jnp.where(hit & ((lane >= half) & (lane < ROT_DIM))[None, :], 1.0, 0.0)
    zero = jnp.zeros_like(cos_m)
    one_row = jnp.zeros((LANES - 4 * half, LANES), F32).at[0].set(jnp.where(lane >= ROT_DIM, 1.0, 0.0))
    rows = [jnp.concatenate([cos_m, zero, zero], axis=1)] * 2 + [jnp.concatenate([zero, slo_m, shi_m], axis=1)] * 2
    rows.append(jnp.concatenate([one_row, jnp.zeros_like(one_row), jnp.zeros_like(one_row)], axis=1))
    return freq, jnp.concatenate(rows, axis=0).astype(BF16)


def kernel(x, positions, ffn1_norm, ffn1_w_gate, ffn1_w_up, ffn1_w_down, mix_norm, w_in, a_ln_gain, a_ln_bias, a_w_s, a_b_s, a_w_proj, b_q_norm, b_k_norm, b_lambda_q1, b_lambda_k1, b_lambda_q2, b_lambda_k2, b_subln, b_w_proj, w_out, ffn2_norm, ffn2_w_gate, ffn2_w_up, ffn2_w_down):
    bsz, seq, _ = x.shape
    depth = w_in.shape[0]
    t = bsz * seq
    xt = x.reshape(t, D_MODEL)
    pos = positions.reshape(t // PROJ_TM, 1, PROJ_TM)
    freq, rope = _rope_tables()
    grp = jnp.arange(MXU_DIM) // B_QK_DIM
    gmat = jnp.where(grp[:, None] == grp[None, :], 1.0 / B_QK_DIM, 0.0).astype(BF16)

    for l in range(depth):
        lam_init = 0.8 - 0.6 * math.exp(-0.3 * l)
        row = lambda a: a[l][None, :].astype(F32)
        xt = _ffn(xt, row(ffn1_norm), ffn1_w_gate[l].astype(BF16), ffn1_w_up[l].astype(BF16),
                  ffn1_w_down[l].astype(BF16))
        ws_pair = a_w_s[l].reshape(A_GROUPS // 2, 2, CHUNK, CHUNK).transpose(0, 2, 1, 3)
        ws_pair = ws_pair.reshape(A_GROUPS // 2, CHUNK, 2 * CHUNK).astype(BF16)
        bs_full = jnp.repeat(a_b_s[l].T, A_GROUP_DIM, axis=1).astype(F32)
        tile = lambda a, scale: jnp.tile(a[l].astype(F32) * scale, B_WIDTH // B_QK_DIM)[None, :]
        ma, sgb, qt, k, vt = _proj(
            xt, pos, row(mix_norm), w_in[l].astype(BF16), row(a_ln_gain), row(a_ln_bias), ws_pair, bs_full,
            a_w_proj[l].astype(BF16), gmat, tile(b_q_norm, B_QK_DIM ** -0.5 * LOG2E), tile(b_k_norm, 1.0), freq, rope, bsz)
        lamp = jnp.stack([b_lambda_q1[l], b_lambda_k1[l], b_lambda_q2[l], b_lambda_k2[l]]).astype(F32)
        subg = jnp.broadcast_to(b_subln[l].astype(F32)[:, None], (B_V_DIM, LANES))
        ybt = _attn(lamp, qt, k.reshape(bsz, seq, B_WIDTH), vt, subg, lam_init)
        xt = _mixffn(xt, ma, sgb, ybt, b_w_proj[l].astype(BF16), w_out[l].astype(BF16), row(ffn2_norm),
                     ffn2_w_gate[l].astype(BF16), ffn2_w_up[l].astype(BF16), ffn2_w_down[l].astype(BF16))
    return xt.reshape(bsz, seq, D_MODEL)
```

```python
import functools
import math

import jax
import jax.numpy as jnp
from jax import lax
from jax.experimental import pallas as pl
from jax.experimental.pallas import tpu as pltpu

D_MODEL = 1024
D_FF = 2816
A_WIDTH = 512
A_GROUPS = 8
A_GROUP_DIM = A_WIDTH // A_GROUPS
CHUNK = 128
B_HEADS = 8
B_QK_DIM = 64
B_V_DIM = 2 * B_QK_DIM
B_WIDTH = B_HEADS * B_V_DIM
ROPE_THETA = 500000.0
ROT_DIM = B_QK_DIM // 4
NORM_EPS = 1e-6
LN_EPS = 1e-5

LANES = 128
MXU_DIM = 256
VMEM_LIMIT = 56 * 1024 * 1024

FFN_TM = 512
PROJ_TM = 512
MIX_TM = 512
ATT_CB = MXU_DIM
ATT_LOOKAHEAD = 5
ATT_ONES_ROWS = 16
LOG2E = math.log2(math.e)

BF16 = jnp.bfloat16
F32 = jnp.float32


def _dot(a, b):
    return jnp.dot(a, b, preferred_element_type=F32)


def _rms_rows(x, gain, eps):
    ms = jnp.mean(x * x, axis=-1, keepdims=True)
    return x * lax.rsqrt(ms + eps) * gain


def _whole_vmem():
    return pl.BlockSpec(memory_space=pltpu.VMEM)


def _params(n_axes):
    return pltpu.CompilerParams(
        dimension_semantics=("arbitrary",) * n_axes,
        vmem_limit_bytes=VMEM_LIMIT,
    )


def _ffn_kernel(x_ref, gain_ref, wg_ref, wu_ref, wd_ref, o_ref, a_ref):
    x = x_ref[...]
    h = _rms_rows(x, gain_ref[...], NORM_EPS).astype(BF16)
    g = _dot(h, wg_ref[...])
    u = _dot(h, wu_ref[...])
    a_ref[...] = (jax.nn.silu(g) * u).astype(BF16)
    y = _dot(a_ref[...], wd_ref[...])
    o_ref[...] = x + 0.5 * y


def _ffn(x, gain, wg, wu, wd):
    t = x.shape[0]
    row = pl.BlockSpec((FFN_TM, D_MODEL), lambda i: (i, 0))
    return pl.pallas_call(
        _ffn_kernel,
        grid=(t // FFN_TM,),
        in_specs=[row, _whole_vmem(), _whole_vmem(), _whole_vmem(), _whole_vmem()],
        out_specs=row,
        out_shape=jax.ShapeDtypeStruct((t, D_MODEL), F32),
        scratch_shapes=[pltpu.VMEM((FFN_TM, D_FF), BF16)],
        compiler_params=_params(1),
        name="ffn",
    )(x, gain, wg, wu, wd)


def _group_mean_sq(z, gmat_ref):
    zz = (z * z).astype(BF16)
    return jnp.concatenate(
        [_dot(zz[:, c:c + MXU_DIM], gmat_ref[...]) for c in range(0, B_WIDTH, MXU_DIM)], axis=1)


def _norm_rope(z, ms, gain, cos_t, sin_lo, sin_hi):
    zn = z * lax.rsqrt(ms + NORM_EPS) * gain
    half = ROT_DIM // 2
    up = pltpu.roll(zn, B_WIDTH - half, 1)
    dn = pltpu.roll(zn, half, 1)
    return (zn * cos_t + up * sin_lo + dn * sin_hi).astype(BF16)


def _proj_kernel(x_ref, pos_ref, gain_ref, win_ref, lng_ref, lnb_ref, ws_ref, bs_ref, pa_ref,
                 gmat_ref, qg_ref, kg_ref, freq_ref, rope_ref,
                 ma_ref, sgb_ref, qt_ref, k_ref, vt_ref):
    tm = x_ref.shape[0]
    h = _rms_rows(x_ref[...], gain_ref[...], NORM_EPS).astype(BF16)

    def zcols(i):
        return _dot(h, win_ref[:, i * D_MODEL:(i + 1) * D_MODEL])

    zq = zcols(3)

    ang = freq_ref[...] * pos_ref[0].astype(F32)
    parts = []
    for tr in (jnp.cos(ang), jnp.sin(ang)):
        hi = tr.astype(BF16).astype(F32)
        parts += [hi, tr - hi]
    parts.append(jnp.ones((LANES - 4 * ROT_DIM // 2, tm), F32))
    trig = jnp.concatenate(parts, axis=0).astype(BF16)
    tab = lax.dot_general(trig, rope_ref[...], (((0,), (0,)), ((), ())), preferred_element_type=F32)

    zk = zcols(4)
    msq = _group_mean_sq(zq, gmat_ref)
    zv = zcols(5)
    msk = _group_mean_sq(zk, gmat_ref)
    zuv = zcols(2)

    reps = B_WIDTH // LANES
    cos_t = jnp.concatenate([tab[:, :LANES]] * reps, axis=1)
    sin_lo = jnp.concatenate([tab[:, LANES:2 * LANES]] * reps, axis=1)
    sin_hi = jnp.concatenate([tab[:, 2 * LANES:]] * reps, axis=1)

    qt_ref[0] = _norm_rope(zq, msq, qg_ref[...], cos_t, sin_lo, sin_hi).T

    zga = zcols(0)

    k_ref[...] = _norm_rope(zk, msk, kg_ref[...], cos_t, sin_lo, sin_hi)
    vt_ref[0] = zv.astype(BF16).T

    zuv = 0.5 * zuv * (1.0 + lax.erf(zuv * (2.0 ** -0.5)))
    u = zuv[:, :A_WIDTH]
    v = zuv[:, A_WIDTH:]
    mu = jnp.mean(v, axis=-1, keepdims=True)
    vc = v - mu
    var = jnp.mean(vc * vc, axis=-1, keepdims=True)
    vn = (vc * lax.rsqrt(var + LN_EPS) * lng_ref[...] + lnb_ref[...]).astype(BF16)

    zgb = zcols(1)

    lane = lax.broadcasted_iota(jnp.int32, (CHUNK, LANES), 1)
    lo_half = lane < A_GROUP_DIM
    row_t = lax.broadcasted_iota(jnp.int32, (CHUNK, 2 * CHUNK), 0)
    col_s = lax.broadcasted_iota(jnp.int32, (CHUNK, 2 * CHUNK), 1) % CHUNK
    causal = col_s <= row_t
    zero = jnp.zeros((), BF16)
    f_rows = []
    for c in range(tm // CHUNK):
        f_cols = []
        for j in range(A_WIDTH // LANES):
            vb = vn[c * CHUNK:(c + 1) * CHUNK, j * LANES:(j + 1) * LANES]
            rhs = jnp.concatenate([jnp.where(lo_half, vb, zero), jnp.where(lo_half, zero, vb)], axis=0)
            wpair = jnp.where(causal, ws_ref[j], zero)
            f_cols.append(_dot(wpair, rhs))
        f_rows.append(jnp.concatenate(f_cols, axis=1) + bs_ref[...])
    f = jnp.concatenate(f_rows, axis=0)

    ya = (u * f).astype(BF16)
    ma_ref[...] = (jax.nn.sigmoid(zga) * _dot(ya, pa_ref[...])).astype(BF16)
    sgb_ref[...] = jax.nn.sigmoid(zgb).astype(BF16)


def _proj(x1, pos, gain, win, lng, lnb, ws_pair, bs_full, pa, gmat, qg, kg, freq, rope, bsz):
    t = x1.shape[0]
    nsb = t // bsz // PROJ_TM
    row = pl.BlockSpec((PROJ_TM, D_MODEL), lambda i: (i, 0))
    col = pl.BlockSpec((1, B_WIDTH, PROJ_TM), lambda i: (i // nsb, 0, i % nsb))
    out = jax.ShapeDtypeStruct((t, D_MODEL), BF16)
    out_t = jax.ShapeDtypeStruct((bsz, B_WIDTH, t // bsz), BF16)
    return pl.pallas_call(
        _proj_kernel,
        grid=(t // PROJ_TM,),
        in_specs=[row, pl.BlockSpec((1, 1, PROJ_TM), lambda i: (i, 0, 0))] + [_whole_vmem()] * 12,
        out_specs=[row, row, col, row, col],
        out_shape=[out, out, out_t, out, out_t],
        compiler_params=_params(1),
        name="proj",
    )(x1, pos, gain, win, lng, lnb, ws_pair, bs_full, pa, gmat, qg, kg, freq, rope)


def _attn_kernel(lamp_ref, qt_ref, k_ref, vt_ref, subg_ref, ot_ref, acc1, acc2, m1, m2, qz_ref, *, lam_init):
    cb = ATT_CB
    seq = qt_ref.shape[2]
    nblk = seq // cb
    lo_rows = lax.broadcasted_iota(jnp.int32, (LANES, seq), 0) < B_QK_DIM
    zero = jnp.zeros((), BF16)
    qt = qt_ref[0]
    qz_ref[0] = jnp.where(lo_rows, qt, zero)
    qz_ref[1] = jnp.where(lo_rows, zero, qt)
    ones = jnp.ones((ATT_ONES_ROWS, cb), BF16)
    krow = lax.broadcasted_iota(jnp.int32, (cb, cb), 0)
    qcol = lax.broadcasted_iota(jnp.int32, (cb, cb), 1)
    diag = krow <= qcol
    stats = ((acc1, m1), (acc2, m2))

    def score(kblk, c, mi):
        return _dot(k_ref[0, kblk * cb:(kblk + 1) * cb, :], qz_ref[mi, :, c * cb:(c + 1) * cb])

    cells = [(kblk, c, mi) for kblk in range(nblk) for c in range(kblk, nblk) for mi in range(2)]
    pending = [score(*cell) for cell in cells[:ATT_LOOKAHEAD]]
    vte = None
    for i, (kblk, c, mi) in enumerate(cells):
        if i + ATT_LOOKAHEAD < len(cells):
            pending.append(score(*cells[i + ATT_LOOKAHEAD]))
        s = pending.pop(0)
        if c == kblk:
            s = jnp.where(diag, s, -jnp.inf)
            if mi == 0:
                vte = jnp.concatenate([vt_ref[0, :, kblk * cb:(kblk + 1) * cb], ones], axis=0)
        acc, m = stats[mi]
        cs = slice(c * cb, (c + 1) * cb)
        bmax = jnp.max(s, axis=0, keepdims=True)
        if kblk == 0:
            m_new = bmax
            acc[:, cs] = _dot(vte, jnp.exp2(s - m_new).astype(BF16))
        else:
            m_old = m[:, cs]
            m_new = jnp.maximum(m_old, bmax)
            alpha = jnp.exp2(m_old - m_new)
            acc[:, cs] = alpha * acc[:, cs] + _dot(vte, jnp.exp2(s - m_new).astype(BF16))
        m[:, cs] = m_new

    lp = lamp_ref[...]
    lam = (jnp.exp(jnp.sum(lp[0:1] * lp[1:2], axis=1, keepdims=True))
           - jnp.exp(jnp.sum(lp[2:3] * lp[3:4], axis=1, keepdims=True)) + lam_init)
    for c in range(nblk):
        cs = slice(c * cb, (c + 1) * cb)
        r1 = 1.0 / acc1[B_V_DIM:B_V_DIM + 1, cs]
        r2 = lam / acc2[B_V_DIM:B_V_DIM + 1, cs]
        ot = acc1[:B_V_DIM, cs] * r1 - acc2[:B_V_DIM, cs] * r2
        ms = jnp.mean(ot * ot, axis=0, keepdims=True)
        gain = jnp.concatenate([subg_ref[...]] * (cb // LANES), axis=1)
        ot_ref[0, :, cs] = (ot * (lax.rsqrt(ms + NORM_EPS) * (1.0 - lam_init)) * gain).astype(BF16)


def _attn(lamp, qt, k, vt, subg, lam_init):
    b, s, _ = k.shape
    tspec = pl.BlockSpec((1, LANES, s), lambda bi, hi: (bi, hi, 0))
    kspec = pl.BlockSpec((1, s, LANES), lambda bi, hi: (bi, 0, hi))
    acc = pltpu.VMEM((B_V_DIM + ATT_ONES_ROWS, s), F32)
    stat = pltpu.VMEM((1, s), F32)
    return pl.pallas_call(
        functools.partial(_attn_kernel, lam_init=lam_init),
        grid=(b, B_HEADS),
        in_specs=[_whole_vmem(), tspec, kspec, tspec, _whole_vmem()],
        out_specs=tspec,
        out_shape=jax.ShapeDtypeStruct((b, B_WIDTH, s), BF16),
        scratch_shapes=[acc, acc, stat, stat, pltpu.VMEM((2, LANES, s), BF16)],
        compiler_params=_params(2),
        name="attn",
    )(lamp, qt, k, vt, subg)


def _mixffn_kernel(x_ref, ma_ref, sgb_ref, ybt_ref, pb_ref, wo_ref, gain_ref, wg_ref, wu_ref, wd_ref, o_ref, a_ref):
    m = ma_ref[...].astype(F32) + sgb_ref[...].astype(F32) * _dot(ybt_ref[0].T, pb_ref[...])
    x = x_ref[...] + _dot(m.astype(BF16), wo_ref[...])
    h = _rms_rows(x, gain_ref[...], NORM_EPS).astype(BF16)
    g = _dot(h, wg_ref[...])
    u = _dot(h, wu_ref[...])
    a_ref[...] = (jax.nn.silu(g) * u).astype(BF16)
    y = _dot(a_ref[...], wd_ref[...])
    o_ref[...] = x + 0.5 * y


def _mixffn(x1, ma, sgb, ybt, pb, wo, gain, wg, wu, wd):
    t = x1.shape[0]
    nsb = ybt.shape[2] // MIX_TM
    row = pl.BlockSpec((MIX_TM, D_MODEL), lambda i: (i, 0))
    col = pl.BlockSpec((1, B_WIDTH, MIX_TM), lambda i: (i // nsb, 0, i % nsb))
    return pl.pallas_call(
        _mixffn_kernel,
        grid=(t // MIX_TM,),
        in_specs=[row, row, row, col] + [_whole_vmem()] * 6,
        out_specs=row,
        out_shape=jax.ShapeDtypeStruct((t, D_MODEL), F32),
        scratch_shapes=[pltpu.VMEM((MIX_TM, D_FF), BF16)],
        compiler_params=_params(1),
        name="mixffn",
    )(x1, ma, sgb, ybt, pb, wo, gain, wg, wu, wd)


def _rope_tables():
    half = ROT_DIM // 2
    freq = (ROPE_THETA ** (-jnp.arange(0, ROT_DIM, 2, dtype=F32) / ROT_DIM))[:, None]
    lane = jnp.arange(LANES) % B_QK_DIM
    hit = (lane[None, :] % half) == jnp.arange(half)[:, None]
    cos_m = jnp.where(hit & (lane < ROT_DIM)[None, :], 1.0, 0.0)
    slo_m = jnp.where(hit & (lane < half)[None, :], -1.0, 0.0)
    shi_m = jnp.where(hit & ((lane >= half) & (lane < ROT_DIM))[None, :], 1.0, 0.0)
    zero = jnp.zeros_like(cos_m)
    one_row = jnp.zeros((LANES - 4 * half, LANES), F32).at[0].set(jnp.where(lane >= ROT_DIM, 1.0, 0.0))
    rows = [jnp.concatenate([cos_m, zero, zero], axis=1)] * 2 + [jnp.concatenate([zero, slo_m, shi_m], axis=1)] * 2
    rows.append(jnp.concatenate([one_row, jnp.zeros_like(one_row), jnp.zeros_like(one_row)], axis=1))
    return freq, jnp.concatenate(rows, axis=0).astype(BF16)


def kernel(x, positions, ffn1_norm, ffn1_w_gate, ffn1_w_up, ffn1_w_down, mix_norm, w_in, a_ln_gain, a_ln_bias, a_w_s, a_b_s, a_w_proj, b_q_norm, b_k_norm, b_lambda_q1, b_lambda_k1, b_lambda_q2, b_lambda_k2, b_subln, b_w_proj, w_out, ffn2_norm, ffn2_w_gate, ffn2_w_up, ffn2_w_down):
    bsz, seq, _ = x.shape
    depth = w_in.shape[0]
    t = bsz * seq
    xt = x.reshape(t, D_MODEL)
    pos = positions.reshape(t // PROJ_TM, 1, PROJ_TM)
    freq, rope = _rope_tables()
    grp = jnp.arange(MXU_DIM) // B_QK_DIM
    gmat = jnp.where(grp[:, None] == grp[None, :], 1.0 / B_QK_DIM, 0.0).astype(BF16)

    for l in range(depth):
        lam_init = 0.8 - 0.6 * math.exp(-0.3 * l)
        row = lambda a: a[l][None, :].astype(F32)
        xt = _ffn(xt, row(ffn1_norm), ffn1_w_gate[l].astype(BF16), ffn1_w_up[l].astype(BF16),
                  ffn1_w_down[l].astype(BF16))
        ws_pair = a_w_s[l].reshape(A_GROUPS // 2, 2, CHUNK, CHUNK).transpose(0, 2, 1, 3)
        ws_pair = ws_pair.reshape(A_GROUPS // 2, CHUNK, 2 * CHUNK).astype(BF16)
        bs_full = jnp.repeat(a_b_s[l].T, A_GROUP_DIM, axis=1).astype(F32)
        tile = lambda a, scale: jnp.tile(a[l].astype(F32) * scale, B_WIDTH // B_QK_DIM)[None, :]
        ma, sgb, qt, k, vt = _proj(
            xt, pos, row(mix_norm), w_in[l].astype(BF16), row(a_ln_gain), row(a_ln_bias), ws_pair, bs_full,
            a_w_proj[l].astype(BF16), gmat, tile(b_q_norm, B_QK_DIM ** -0.5 * LOG2E), tile(b_k_norm, 1.0), freq, rope, bsz)
        lamp = jnp.stack([b_lambda_q1[l], b_lambda_k1[l], b_lambda_q2[l], b_lambda_k2[l]]).astype(F32)
        subg = jnp.broadcast_to(b_subln[l].astype(F32)[:, None], (B_V_DIM, LANES))
        ybt = _attn(lamp, qt, k.reshape(bsz, seq, B_WIDTH), vt, subg, lam_init)
        xt = _mixffn(xt, ma, sgb, ybt, b_w_proj[l].astype(BF16), w_out[l].astype(BF16), row(ffn2_norm),
                     ffn2_w_gate[l].astype(BF16), ffn2_w_up[l].astype(BF16), ffn2_w_down[l].astype(BF16))
    return xt.reshape(bsz, seq, D_MODEL)
```

```python
import functools
import math

import jax
import jax.numpy as jnp
from jax import lax
from jax.experimental import pallas as pl
from jax.experimental.pallas import tpu as pltpu

D_MODEL = 1024
D_FF = 2816
A_WIDTH = 512
A_GROUPS = 8
A_GROUP_DIM = A_WIDTH // A_GROUPS
CHUNK = 128
B_HEADS = 8
B_QK_DIM = 64
B_V_DIM = 2 * B_QK_DIM
B_WIDTH = B_HEADS * B_V_DIM
ROPE_THETA = 500000.0
ROT_DIM = B_QK_DIM // 4
NORM_EPS = 1e-6
LN_EPS = 1e-5

LANES = 128
MXU_DIM = 256
VMEM_LIMIT = 56 * 1024 * 1024

FFN_TM = 512
FFN_LEAD = MXU_DIM
MIXFFN_F_SPLIT = 6 * MXU_DIM
PROJ_TM = 512
MIX_TM = 512
ATT_CB = MXU_DIM
ATT_LOOKAHEAD = 5
ATT_ONES_ROWS = 16
LOG2E = math.log2(math.e)

BF16 = jnp.bfloat16
F32 = jnp.float32


def _dot(a, b):
    return jnp.dot(a, b, preferred_element_type=F32)


def _rms_rows(x, gain, eps):
    ms = jnp.mean(x * x, axis=-1, keepdims=True)
    return x * lax.rsqrt(ms + eps) * gain


def _whole_vmem():
    return pl.BlockSpec(memory_space=pltpu.VMEM)


def _params(n_axes):
    return pltpu.CompilerParams(
        dimension_semantics=("arbitrary",) * n_axes,
        vmem_limit_bytes=VMEM_LIMIT,
    )


def _ffn_kernel(x_ref, xn_ref, gain_ref, wg_ref, wu_ref, wd_ref, o_ref, a_ref, h_ref, g0_ref):
    def lead(src_ref):
        h = _rms_rows(src_ref[...], gain_ref[...], NORM_EPS).astype(BF16)
        h_ref[...] = h
        g0_ref[...] = _dot(h, wg_ref[:, :FFN_LEAD])

    @pl.when(pl.program_id(0) == 0)
    def _():
        lead(x_ref)

    h = h_ref[...]
    u = _dot(h, wu_ref[...])
    g = _dot(h, wg_ref[:, FFN_LEAD:])
    a_ref[:, :FFN_LEAD] = (jax.nn.silu(g0_ref[...]) * u[:, :FFN_LEAD]).astype(BF16)
    a_ref[:, FFN_LEAD:] = (jax.nn.silu(g) * u[:, FFN_LEAD:]).astype(BF16)
    y = _dot(a_ref[...], wd_ref[...])
    o_ref[...] = x_ref[...] + 0.5 * y
    lead(xn_ref)


def _ffn(x, gain, wg, wu, wd):
    t = x.shape[0]
    n = t // FFN_TM
    row = pl.BlockSpec((FFN_TM, D_MODEL), lambda i: (i, 0))
    nxt = pl.BlockSpec((FFN_TM, D_MODEL), lambda i: (jnp.minimum(i + 1, n - 1), 0))
    return pl.pallas_call(
        _ffn_kernel,
        grid=(n,),
        in_specs=[row, nxt, _whole_vmem(), _whole_vmem(), _whole_vmem(), _whole_vmem()],
        out_specs=row,
        out_shape=jax.ShapeDtypeStruct((t, D_MODEL), F32),
        scratch_shapes=[pltpu.VMEM((FFN_TM, D_FF), BF16), pltpu.VMEM((FFN_TM, D_MODEL), BF16),
                        pltpu.VMEM((FFN_TM, FFN_LEAD), F32)],
        compiler_params=_params(1),
        name="ffn",
    )(x, x, gain, wg, wu, wd)


def _group_mean_sq(z, gmat_ref):
    zz = (z * z).astype(BF16)
    return jnp.concatenate(
        [_dot(zz[:, c:c + MXU_DIM], gmat_ref[...]) for c in range(0, B_WIDTH, MXU_DIM)], axis=1)


def _norm_rope(z, ms, gain, cos_t, sin_lo, sin_hi):
    zn = z * lax.rsqrt(ms + NORM_EPS) * gain
    half = ROT_DIM // 2
    up = pltpu.roll(zn, B_WIDTH - half, 1)
    dn = pltpu.roll(zn, half, 1)
    return (zn * cos_t + up * sin_lo + dn * sin_hi).astype(BF16)


def _proj_kernel(x_ref, pos_ref, gain_ref, win_ref, lng_ref, lnb_ref, ws_ref, bs_ref, pa_ref,
                 gmat_ref, qg_ref, kg_ref, freq_ref, rope_ref,
                 ma_ref, sgb_ref, qt_ref, k_ref, vt_ref):
    tm = x_ref.shape[0]
    h = _rms_rows(x_ref[...], gain_ref[...], NORM_EPS).astype(BF16)

    def zcols(i):
        return _dot(h, win_ref[:, i * D_MODEL:(i + 1) * D_MODEL])

    zq = zcols(3)

    ang = freq_ref[...] * pos_ref[0].astype(F32)
    parts = []
    for tr in (jnp.cos(ang), jnp.sin(ang)):
        hi = tr.astype(BF16).astype(F32)
        parts += [hi, tr - hi]
    parts.append(jnp.ones((LANES - 4 * ROT_DIM // 2, tm), F32))
    trig = jnp.concatenate(parts, axis=0).astype(BF16)
    tab = lax.dot_general(trig, rope_ref[...], (((0,), (0,)), ((), ())), preferred_element_type=F32)

    zk = zcols(4)
    msq = _group_mean_sq(zq, gmat_ref)
    zv = zcols(5)
    msk = _group_mean_sq(zk, gmat_ref)
    zuv = zcols(2)

    reps = B_WIDTH // LANES
    cos_t = jnp.concatenate([tab[:, :LANES]] * reps, axis=1)
    sin_lo = jnp.concatenate([tab[:, LANES:2 * LANES]] * reps, axis=1)
    sin_hi = jnp.concatenate([tab[:, 2 * LANES:]] * reps, axis=1)

    qt_ref[0] = _norm_rope(zq, msq, qg_ref[...], cos_t, sin_lo, sin_hi).T

    zga = zcols(0)

    k_ref[...] = _norm_rope(zk, msk, kg_ref[...], cos_t, sin_lo, sin_hi)
    vt_ref[0] = zv.astype(BF16).T

    zuv = 0.5 * zuv * (1.0 + lax.erf(zuv * (2.0 ** -0.5)))
    u = zuv[:, :A_WIDTH]
    v = zuv[:, A_WIDTH:]
    mu = jnp.mean(v, axis=-1, keepdims=True)
    vc = v - mu
    var = jnp.mean(vc * vc, axis=-1, keepdims=True)
    vn = (vc * lax.rsqrt(var + LN_EPS) * lng_ref[...] + lnb_ref[...]).astype(BF16)

    zgb = zcols(1)

    lane = lax.broadcasted_iota(jnp.int32, (CHUNK, LANES), 1)
    lo_half = lane < A_GROUP_DIM
    row_t = lax.broadcasted_iota(jnp.int32, (CHUNK, 2 * CHUNK), 0)
    col_s = lax.broadcasted_iota(jnp.int32, (CHUNK, 2 * CHUNK), 1) % CHUNK
    causal = col_s <= row_t
    zero = jnp.zeros((), BF16)
    f_rows = []
    for c in range(tm // CHUNK):
        f_cols = []
        for j in range(A_WIDTH // LANES):
            vb = vn[c * CHUNK:(c + 1) * CHUNK, j * LANES:(j + 1) * LANES]
            rhs = jnp.concatenate([jnp.where(lo_half, vb, zero), jnp.where(lo_half, zero, vb)], axis=0)
            wpair = jnp.where(causal, ws_ref[j], zero)
            f_cols.append(_dot(wpair, rhs))
        f_rows.append(jnp.concatenate(f_cols, axis=1) + bs_ref[...])
    f = jnp.concatenate(f_rows, axis=0)

    ya = (u * f).astype(BF16)
    ma_ref[...] = (jax.nn.sigmoid(zga) * _dot(ya, pa_ref[...])).astype(BF16)
    sgb_ref[...] = jax.nn.sigmoid(zgb).astype(BF16)


def _proj(x1, pos, gain, win, lng, lnb, ws_pair, bs_full, pa, gmat, qg, kg, freq, rope, bsz):
    t = x1.shape[0]
    nsb = t // bsz // PROJ_TM
    row = pl.BlockSpec((PROJ_TM, D_MODEL), lambda i: (i, 0))
    col = pl.BlockSpec((1, B_WIDTH, PROJ_TM), lambda i: (i // nsb, 0, i % nsb))
    out = jax.ShapeDtypeStruct((t, D_MODEL), BF16)
    out_t = jax.ShapeDtypeStruct((bsz, B_WIDTH, t // bsz), BF16)
    return pl.pallas_call(
        _proj_kernel,
        grid=(t // PROJ_TM,),
        in_specs=[row, pl.BlockSpec((1, 1, PROJ_TM), lambda i: (i, 0, 0))] + [_whole_vmem()] * 12,
        out_specs=[row, row, col, row, col],
        out_shape=[out, out, out_t, out, out_t],
        compiler_params=_params(1),
        name="proj",
    )(x1, pos, gain, win, lng, lnb, ws_pair, bs_full, pa, gmat, qg, kg, freq, rope)


def _attn_kernel(lamp_ref, qt_ref, k_ref, vt_ref, subg_ref, ot_ref, acc1, acc2, m1, m2, qz_ref, *, lam_init):
    cb = ATT_CB
    seq = qt_ref.shape[2]
    nblk = seq // cb
    lo_rows = lax.broadcasted_iota(jnp.int32, (LANES, seq), 0) < B_QK_DIM
    zero = jnp.zeros((), BF16)
    qt = qt_ref[0]
    qz_ref[0] = jnp.where(lo_rows, qt, zero)
    qz_ref[1] = jnp.where(lo_rows, zero, qt)
    ones = jnp.ones((ATT_ONES_ROWS, cb), BF16)
    krow = lax.broadcasted_iota(jnp.int32, (cb, cb), 0)
    qcol = lax.broadcasted_iota(jnp.int32, (cb, cb), 1)
    diag = krow <= qcol
    stats = ((acc1, m1), (acc2, m2))

    def score(kblk, c, mi):
        return _dot(k_ref[0, kblk * cb:(kblk + 1) * cb, :], qz_ref[mi, :, c * cb:(c + 1) * cb])

    cells = [(kblk, c, mi) for kblk in range(nblk) for c in range(kblk, nblk) for mi in range(2)]
    pending = [score(*cell) for cell in cells[:ATT_LOOKAHEAD]]
    vte = None
    for i, (kblk, c, mi) in enumerate(cells):
        if i + ATT_LOOKAHEAD < len(cells):
            pending.append(score(*cells[i + ATT_LOOKAHEAD]))
        s = pending.pop(0)
        if c == kblk:
            s = jnp.where(diag, s, -jnp.inf)
            if mi == 0:
                vte = jnp.concatenate([vt_ref[0, :, kblk * cb:(kblk + 1) * cb], ones], axis=0)
        acc, m = stats[mi]
        cs = slice(c * cb, (c + 1) * cb)
        bmax = jnp.max(s, axis=0, keepdims=True)
        if kblk == 0:
            m_new = bmax
            acc[:, cs] = _dot(vte, jnp.exp2(s - m_new).astype(BF16))
        else:
            m_old = m[:, cs]
            m_new = jnp.maximum(m_old, bmax)
            alpha = jnp.exp2(m_old - m_new)
            acc[:, cs] = alpha * acc[:, cs] + _dot(vte, jnp.exp2(s - m_new).astype(BF16))
        m[:, cs] = m_new

    lp = lamp_ref[...]
    lam = (jnp.exp(jnp.sum(lp[0:1] * lp[1:2], axis=1, keepdims=True))
           - jnp.exp(jnp.sum(lp[2:3] * lp[3:4], axis=1, keepdims=True)) + lam_init)
    for c in range(nblk):
        cs = slice(c * cb, (c + 1) * cb)
        r1 = 1.0 / acc1[B_V_DIM:B_V_DIM + 1, cs]
        r2 = lam / acc2[B_V_DIM:B_V_DIM + 1, cs]
        ot = acc1[:B_V_DIM, cs] * r1 - acc2[:B_V_DIM, cs] * r2
        ms = jnp.mean(ot * ot, axis=0, keepdims=True)
        gain = jnp.concatenate([subg_ref[...]] * (cb // LANES), axis=1)
        ot_ref[0, :, cs] = (ot * (lax.rsqrt(ms + NORM_EPS) * (1.0 - lam_init)) * gain).astype(BF16)


def _attn(lamp, qt, k, vt, subg, lam_init):
    b, s, _ = k.shape
    tspec = pl.BlockSpec((1, LANES, s), lambda bi, hi: (bi, hi, 0))
    kspec = pl.BlockSpec((1, s, LANES), lambda bi, hi: (bi, 0, hi))
    acc = pltpu.VMEM((B_V_DIM + ATT_ONES_ROWS, s), F32)
    stat = pltpu.VMEM((1, s), F32)
    return pl.pallas_call(
        functools.partial(_attn_kernel, lam_init=lam_init),
        grid=(b, B_HEADS),
        in_specs=[_whole_vmem(), tspec, kspec, tspec, _whole_vmem()],
        out_specs=tspec,
        out_shape=jax.ShapeDtypeStruct((b, B_WIDTH, s), BF16),
        scratch_shapes=[acc, acc, stat, stat, pltpu.VMEM((2, LANES, s), BF16)],
        compiler_params=_params(2),
        name="attn",
    )(lamp, qt, k, vt, subg)


def _mixffn_kernel(x0_ref, ma0_ref, sgb0_ref, ybt0_ref, xn_ref, man_ref, sgbn_ref, ybtn_ref,
                   pb_ref, wo_ref, gain_ref, wg_ref, wu_ref, wd_ref, o_ref, a_ref, h_ref, g0_ref, x2_ref):
    def mix_a(ybt_ref):
        return _dot(ybt_ref[0].T, pb_ref[...])

    def mix_b(ma_ref, sgb_ref, ypb):
        m = ma_ref[...].astype(F32) + sgb_ref[...].astype(F32) * ypb
        return _dot(m.astype(BF16), wo_ref[...])

    def lead(x1_ref, mo):
        x2 = x1_ref[...] + mo
        x2_ref[...] = x2
        h = _rms_rows(x2, gain_ref[...], NORM_EPS).astype(BF16)
        h_ref[...] = h
        g0_ref[...] = _dot(h, wg_ref[:, :FFN_LEAD])

    @pl.when(pl.program_id(0) == 0)
    def _():
        lead(x0_ref, mix_b(ma0_ref, sgb0_ref, mix_a(ybt0_ref)))

    h = h_ref[...]
    a_ref[:, :FFN_LEAD] = (jax.nn.silu(g0_ref[...]) * _dot(h, wu_ref[:, :FFN_LEAD])).astype(BF16)
    for c0, c1 in ((FFN_LEAD, MIXFFN_F_SPLIT), (MIXFFN_F_SPLIT, D_FF)):
        g = _dot(h, wg_ref[:, c0:c1])
        u = _dot(h, wu_ref[:, c0:c1])
        a_ref[:, c0:c1] = (jax.nn.silu(g) * u).astype(BF16)
    half = D_MODEL // 2
    ypb = mix_a(ybtn_ref)
    y_lo = _dot(a_ref[...], wd_ref[:, :half])
    mo = mix_b(man_ref, sgbn_ref, ypb)
    y_hi = _dot(a_ref[...], wd_ref[:, half:])
    o_ref[:, :half] = x2_ref[:, :half] + 0.5 * y_lo
    o_ref[:, half:] = x2_ref[:, half:] + 0.5 * y_hi
    lead(xn_ref, mo)


def _mixffn(x1, ma, sgb, ybt, pb, wo, gain, wg, wu, wd):
    t = x1.shape[0]
    n = t // MIX_TM
    nsb = ybt.shape[2] // MIX_TM
    once = pl.Buffered(1)
    row0 = pl.BlockSpec((MIX_TM, D_MODEL), lambda i: (0, 0), pipeline_mode=once)
    col0 = pl.BlockSpec((1, B_WIDTH, MIX_TM), lambda i: (0, 0, 0), pipeline_mode=once)
    nxt = lambda i: jnp.minimum(i + 1, n - 1)
    rown = pl.BlockSpec((MIX_TM, D_MODEL), lambda i: (nxt(i), 0))
    coln = pl.BlockSpec((1, B_WIDTH, MIX_TM), lambda i: (nxt(i) // nsb, 0, nxt(i) % nsb))
    return pl.pallas_call(
        _mixffn_kernel,
        grid=(n,),
        in_specs=[row0, row0, row0, col0, rown, rown, rown, coln] + [_whole_vmem()] * 6,
        out_specs=pl.BlockSpec((MIX_TM, D_MODEL), lambda i: (i, 0)),
        out_shape=jax.ShapeDtypeStruct((t, D_MODEL), F32),
        scratch_shapes=[pltpu.VMEM((MIX_TM, D_FF), BF16), pltpu.VMEM((MIX_TM, D_MODEL), BF16),
                        pltpu.VMEM((MIX_TM, FFN_LEAD), F32), pltpu.VMEM((MIX_TM, D_MODEL), F32)],
        compiler_params=_params(1),
        name="mixffn",
    )(x1, ma, sgb, ybt, x1, ma, sgb, ybt, pb, wo, gain, wg, wu, wd)


def _rope_tables():
    half = ROT_DIM // 2
    freq = (ROPE_THETA ** (-jnp.arange(0, ROT_DIM, 2, dtype=F32) / ROT_DIM))[:, None]
    lane = jnp.arange(LANES) % B_QK_DIM
    hit = (lane[None, :] % half) == jnp.arange(half)[:, None]
    cos_m = jnp.where(hit & (lane < ROT_DIM)[None, :], 1.0, 0.0)
    slo_m = jnp.where(hit & (lane < half)[None, :], -1.0, 0.0)
    shi_m = jnp.where(hit & ((lane >= half) & (lane < ROT_DIM))[None, :], 1.0, 0.0)
    zero = jnp.zeros_like(cos_m)
    one_row = jnp.zeros((LANES - 4 * half, LANES), F32).at[0].set(jnp.where(lane >= ROT_DIM, 1.0, 0.0))
    rows = [jnp.concatenate([cos_m, zero, zero], axis=1)] * 2 + [jnp.concatenate([zero, slo_m, shi_m], axis=1)] * 2
    rows.append(jnp.concatenate([one_row, jnp.zeros_like(one_row), jnp.zeros_like(one_row)], axis=1))
    return freq, jnp.concatenate(rows, axis=0).astype(BF16)


def kernel(x, positions, ffn1_norm, ffn1_w_gate, ffn1_w_up, ffn1_w_down, mix_norm, w_in, a_ln_gain, a_ln_bias, a_w_s, a_b_s, a_w_proj, b_q_norm, b_k_norm, b_lambda_q1, b_lambda_k1, b_lambda_q2, b_lambda_k2, b_subln, b_w_proj, w_out, ffn2_norm, ffn2_w_gate, ffn2_w_up, ffn2_w_down):
    bsz, seq, _ = x.shape
    depth = w_in.shape[0]
    t = bsz * seq
    xt = x.reshape(t, D_MODEL)
    pos = positions.reshape(t // PROJ_TM, 1, PROJ_TM)
    freq, rope = _rope_tables()
    grp = jnp.arange(MXU_DIM) // B_QK_DIM
    gmat = jnp.where(grp[:, None] == grp[None, :], 1.0 / B_QK_DIM, 0.0).astype(BF16)

    for l in range(depth):
        lam_init = 0.8 - 0.6 * math.exp(-0.3 * l)
        row = lambda a: a[l][None, :].astype(F32)
        xt = _ffn(xt, row(ffn1_norm), ffn1_w_gate[l].astype(BF16), ffn1_w_up[l].astype(BF16),
                  ffn1_w_down[l].astype(BF16))
        ws_pair = a_w_s[l].reshape(A_GROUPS // 2, 2, CHUNK, CHUNK).transpose(0, 2, 1, 3)
        ws_pair = ws_pair.reshape(A_GROUPS // 2, CHUNK, 2 * CHUNK).astype(BF16)
        bs_full = jnp.repeat(a_b_s[l].T, A_GROUP_DIM, axis=1).astype(F32)
        tile = lambda a, scale: jnp.tile(a[l].astype(F32) * scale, B_WIDTH // B_QK_DIM)[None, :]
        ma, sgb, qt, k, vt = _proj(
            xt, pos, row(mix_norm), w_in[l].astype(BF16), row(a_ln_gain), row(a_ln_bias), ws_pair, bs_full,
            a_w_proj[l].astype(BF16), gmat, tile(b_q_norm, B_QK_DIM ** -0.5 * LOG2E), tile(b_k_norm, 1.0), freq, rope, bsz)
        lamp = jnp.stack([b_lambda_q1[l], b_lambda_k1[l], b_lambda_q2[l], b_lambda_k2[l]]).astype(F32)
        subg = jnp.broadcast_to(b_subln[l].astype(F32)[:, None], (B_V_DIM, LANES))
        ybt = _attn(lamp, qt, k.reshape(bsz, seq, B_WIDTH), vt, subg, lam_init)
        xt = _mixffn(xt, ma, sgb, ybt, b_w_proj[l].astype(BF16), w_out[l].astype(BF16), row(ffn2_norm),
                     ffn2_w_gate[l].astype(BF16), ffn2_w_up[l].astype(BF16), ffn2_w_down[l].astype(BF16))
    return xt.reshape(bsz, seq, D_MODEL)
```

```python
import functools
import math

import jax
import jax.numpy as jnp
from jax import lax
from jax.experimental import pallas as pl
from jax.experimental.pallas import tpu as pltpu

D_MODEL = 1024
D_FF = 2816
A_WIDTH = 512
A_GROUPS = 8
A_GROUP_DIM = A_WIDTH // A_GROUPS
CHUNK = 128
B_HEADS = 8
B_QK_DIM = 64
B_V_DIM = 2 * B_QK_DIM
B_WIDTH = B_HEADS * B_V_DIM
ROPE_THETA = 500000.0
ROT_DIM = B_QK_DIM // 4
NORM_EPS = 1e-6
LN_EPS = 1e-5

LANES = 128
BF16_SUBLANES = 16
MXU_DIM = 256
VMEM_LIMIT = 56 * 1024 * 1024

FFN_TM = 512
FFN_LEAD = MXU_DIM
MIXFFN_F_SPLIT = 6 * MXU_DIM
PROJ_TM = 512
MIX_TM = 512
ATT_CB = MXU_DIM
ATT_LOOKAHEAD = 5
ATT_ONES_ROWS = 16
LOG2E = math.log2(math.e)

BF16 = jnp.bfloat16
F32 = jnp.float32


def _dot(a, b):
    return jnp.dot(a, b, preferred_element_type=F32)


def _rms_rows(x, gain, eps):
    ms = jnp.mean(x * x, axis=-1, keepdims=True)
    return x * lax.rsqrt(ms + eps) * gain


def _whole_vmem():
    return pl.BlockSpec(memory_space=pltpu.VMEM)


def _params(n_axes):
    return pltpu.CompilerParams(
        dimension_semantics=("arbitrary",) * n_axes,
        vmem_limit_bytes=VMEM_LIMIT,
    )


def _ffn_kernel(x_ref, xn_ref, gain_ref, wg_ref, wu_ref, wd_ref, *rest):
    ncast = (len(rest) - 4) // 2
    srcs, o_ref, dsts = rest[:ncast], rest[ncast], rest[ncast + 1:2 * ncast + 1]
    a_ref, h_ref, g0_ref = rest[2 * ncast + 1:]
    for src, dst in zip(srcs, dsts):
        dst[...] = src[...].astype(BF16)

    def lead(src_ref):
        h = _rms_rows(src_ref[...], gain_ref[...], NORM_EPS).astype(BF16)
        h_ref[...] = h
        g0_ref[...] = _dot(h, wg_ref[:, :FFN_LEAD])

    @pl.when(pl.program_id(0) == 0)
    def _():
        lead(x_ref)

    h = h_ref[...]
    u = _dot(h, wu_ref[...])
    g = _dot(h, wg_ref[:, FFN_LEAD:])
    a_ref[:, :FFN_LEAD] = (jax.nn.silu(g0_ref[...]) * u[:, :FFN_LEAD]).astype(BF16)
    a_ref[:, FFN_LEAD:] = (jax.nn.silu(g) * u[:, FFN_LEAD:]).astype(BF16)
    y = _dot(a_ref[...], wd_ref[...])
    o_ref[...] = x_ref[...] + 0.5 * y
    lead(xn_ref)


def _slab_spec(w, n):
    rows = next(r for r in range(BF16_SUBLANES, w.shape[0] + 1, BF16_SUBLANES)
                if w.shape[0] % r == 0 and r * n >= w.shape[0])
    last = w.shape[0] // rows - 1
    return pl.BlockSpec((rows, w.shape[1]), lambda i: (jnp.minimum(i, last), 0))


def _ffn(x, gain, wg, wu, wd, later_weights):
    t = x.shape[0]
    n = t // FFN_TM
    row = pl.BlockSpec((FFN_TM, D_MODEL), lambda i: (i, 0))
    nxt = pl.BlockSpec((FFN_TM, D_MODEL), lambda i: (jnp.minimum(i + 1, n - 1), 0))
    slabs = [_slab_spec(w, n) for w in later_weights]
    return pl.pallas_call(
        _ffn_kernel,
        grid=(n,),
        in_specs=[row, nxt, _whole_vmem(), _whole_vmem(), _whole_vmem(), _whole_vmem()] + slabs,
        out_specs=[row] + slabs,
        out_shape=[jax.ShapeDtypeStruct((t, D_MODEL), F32)]
                  + [jax.ShapeDtypeStruct(w.shape, BF16) for w in later_weights],
        scratch_shapes=[pltpu.VMEM((FFN_TM, D_FF), BF16), pltpu.VMEM((FFN_TM, D_MODEL), BF16),
                        pltpu.VMEM((FFN_TM, FFN_LEAD), F32)],
        compiler_params=_params(1),
        name="ffn",
    )(x, x, gain, wg, wu, wd, *later_weights)


def _group_mean_sq(z, gmat_ref):
    zz = (z * z).astype(BF16)
    return jnp.concatenate(
        [_dot(zz[:, c:c + MXU_DIM], gmat_ref[...]) for c in range(0, B_WIDTH, MXU_DIM)], axis=1)


def _norm_rope(z, ms, gain, cos_t, sin_lo, sin_hi):
    zn = z * lax.rsqrt(ms + NORM_EPS) * gain
    half = ROT_DIM // 2
    up = pltpu.roll(zn, B_WIDTH - half, 1)
    dn = pltpu.roll(zn, half, 1)
    return (zn * cos_t + up * sin_lo + dn * sin_hi).astype(BF16)


def _proj_kernel(x_ref, pos_ref, gain_ref, win_ref, lng_ref, lnb_ref, ws_ref, bs_ref, pa_ref,
                 gmat_ref, qg_ref, kg_ref, freq_ref, rope_ref,
                 ma_ref, sgb_ref, qt_ref, k_ref, vt_ref):
    tm = x_ref.shape[0]
    h = _rms_rows(x_ref[...], gain_ref[...], NORM_EPS).astype(BF16)

    def zcols(i):
        return _dot(h, win_ref[:, i * D_MODEL:(i + 1) * D_MODEL])

    zq = zcols(3)

    ang = freq_ref[...] * pos_ref[0].astype(F32)
    parts = []
    for tr in (jnp.cos(ang), jnp.sin(ang)):
        hi = tr.astype(BF16).astype(F32)
        parts += [hi, tr - hi]
    parts.append(jnp.ones((LANES - 4 * ROT_DIM // 2, tm), F32))
    trig = jnp.concatenate(parts, axis=0).astype(BF16)
    tab = lax.dot_general(trig, rope_ref[...], (((0,), (0,)), ((), ())), preferred_element_type=F32)

    zk = zcols(4)
    msq = _group_mean_sq(zq, gmat_ref)
    zv = zcols(5)
    msk = _group_mean_sq(zk, gmat_ref)
    zuv = zcols(2)

    reps = B_WIDTH // LANES
    cos_t = jnp.concatenate([tab[:, :LANES]] * reps, axis=1)
    sin_lo = jnp.concatenate([tab[:, LANES:2 * LANES]] * reps, axis=1)
    sin_hi = jnp.concatenate([tab[:, 2 * LANES:]] * reps, axis=1)

    qt_ref[0] = _norm_rope(zq, msq, qg_ref[...], cos_t, sin_lo, sin_hi).T

    zga = zcols(0)

    k_ref[...] = _norm_rope(zk, msk, kg_ref[...], cos_t, sin_lo, sin_hi)
    vt_ref[0] = zv.astype(BF16).T

    zuv = 0.5 * zuv * (1.0 + lax.erf(zuv * (2.0 ** -0.5)))
    u = zuv[:, :A_WIDTH]
    v = zuv[:, A_WIDTH:]
    mu = jnp.mean(v, axis=-1, keepdims=True)
    vc = v - mu
    var = jnp.mean(vc * vc, axis=-1, keepdims=True)
    vn = (vc * lax.rsqrt(var + LN_EPS) * lng_ref[...] + lnb_ref[...]).astype(BF16)

    zgb = zcols(1)

    lane = lax.broadcasted_iota(jnp.int32, (CHUNK, LANES), 1)
    lo_half = lane < A_GROUP_DIM
    row_t = lax.broadcasted_iota(jnp.int32, (CHUNK, 2 * CHUNK), 0)
    col_s = lax.broadcasted_iota(jnp.int32, (CHUNK, 2 * CHUNK), 1) % CHUNK
    causal = col_s <= row_t
    zero = jnp.zeros((), BF16)
    f_rows = []
    for c in range(tm // CHUNK):
        f_cols = []
        for j in range(A_WIDTH // LANES):
            vb = vn[c * CHUNK:(c + 1) * CHUNK, j * LANES:(j + 1) * LANES]
            rhs = jnp.concatenate([jnp.where(lo_half, vb, zero), jnp.where(lo_half, zero, vb)], axis=0)
            wpair = jnp.where(causal, ws_ref[j], zero)
            f_cols.append(_dot(wpair, rhs))
        f_rows.append(jnp.concatenate(f_cols, axis=1) + bs_ref[...])
    f = jnp.concatenate(f_rows, axis=0)

    ya = (u * f).astype(BF16)
    ma_ref[...] = (jax.nn.sigmoid(zga) * _dot(ya, pa_ref[...])).astype(BF16)
    sgb_ref[...] = jax.nn.sigmoid(zgb).astype(BF16)


def _proj(x1, pos, gain, win, lng, lnb, ws_pair, bs_full, pa, gmat, qg, kg, freq, rope, bsz):
    t = x1.shape[0]
    nsb = t // bsz // PROJ_TM
    row = pl.BlockSpec((PROJ_TM, D_MODEL), lambda i: (i, 0))
    col = pl.BlockSpec((1, B_WIDTH, PROJ_TM), lambda i: (i // nsb, 0, i % nsb))
    out = jax.ShapeDtypeStruct((t, D_MODEL), BF16)
    out_t = jax.ShapeDtypeStruct((bsz, B_WIDTH, t // bsz), BF16)
    return pl.pallas_call(
        _proj_kernel,
        grid=(t // PROJ_TM,),
        in_specs=[row, pl.BlockSpec((1, 1, PROJ_TM), lambda i: (i, 0, 0))] + [_whole_vmem()] * 12,
        out_specs=[row, row, col, row, col],
        out_shape=[out, out, out_t, out, out_t],
        compiler_params=_params(1),
        name="proj",
    )(x1, pos, gain, win, lng, lnb, ws_pair, bs_full, pa, gmat, qg, kg, freq, rope)


def _attn_kernel(lamp_ref, qt_ref, k_ref, vt_ref, subg_ref, ot_ref, acc1, acc2, m1, m2, qz_ref, *, lam_init):
    cb = ATT_CB
    seq = qt_ref.shape[2]
    nblk = seq // cb
    lo_rows = lax.broadcasted_iota(jnp.int32, (LANES, seq), 0) < B_QK_DIM
    zero = jnp.zeros((), BF16)
    qt = qt_ref[0]
    qz_ref[0] = jnp.where(lo_rows, qt, zero)
    qz_ref[1] = jnp.where(lo_rows, zero, qt)
    ones = jnp.ones((ATT_ONES_ROWS, cb), BF16)
    krow = lax.broadcasted_iota(jnp.int32, (cb, cb), 0)
    qcol = lax.broadcasted_iota(jnp.int32, (cb, cb), 1)
    diag = krow <= qcol
    stats = ((acc1, m1), (acc2, m2))

    def score(kblk, c, mi):
        return _dot(k_ref[0, kblk * cb:(kblk + 1) * cb, :], qz_ref[mi, :, c * cb:(c + 1) * cb])

    cells = [(kblk, c, mi) for kblk in range(nblk) for c in range(kblk, nblk) for mi in range(2)]
    pending = [score(*cell) for cell in cells[:ATT_LOOKAHEAD]]
    vte = None
    for i, (kblk, c, mi) in enumerate(cells):
        if i + ATT_LOOKAHEAD < len(cells):
            pending.append(score(*cells[i + ATT_LOOKAHEAD]))
        s = pending.pop(0)
        if c == kblk:
            s = jnp.where(diag, s, -jnp.inf)
            if mi == 0:
                vte = jnp.concatenate([vt_ref[0, :, kblk * cb:(kblk + 1) * cb], ones], axis=0)
        acc, m = stats[mi]
        cs = slice(c * cb, (c + 1) * cb)
        bmax = jnp.max(s, axis=0, keepdims=True)
        if kblk == 0:
            m_new = bmax
            acc[:, cs] = _dot(vte, jnp.exp2(s - m_new).astype(BF16))
        else:
            m_old = m[:, cs]
            m_new = jnp.maximum(m_old, bmax)
            alpha = jnp.exp2(m_old - m_new)
            acc[:, cs] = alpha * acc[:, cs] + _dot(vte, jnp.exp2(s - m_new).astype(BF16))
        m[:, cs] = m_new

    lp = lamp_ref[...]
    lam = (jnp.exp(jnp.sum(lp[0:1] * lp[1:2], axis=1, keepdims=True))
           - jnp.exp(jnp.sum(lp[2:3] * lp[3:4], axis=1, keepdims=True)) + lam_init)
    for c in range(nblk):
        cs = slice(c * cb, (c + 1) * cb)
        r1 = 1.0 / acc1[B_V_DIM:B_V_DIM + 1, cs]
        r2 = lam / acc2[B_V_DIM:B_V_DIM + 1, cs]
        ot = acc1[:B_V_DIM, cs] * r1 - acc2[:B_V_DIM, cs] * r2
        ms = jnp.mean(ot * ot, axis=0, keepdims=True)
        gain = jnp.concatenate([subg_ref[...]] * (cb // LANES), axis=1)
        ot_ref[0, :, cs] = (ot * (lax.rsqrt(ms + NORM_EPS) * (1.0 - lam_init)) * gain).astype(BF16)


def _attn(lamp, qt, k, vt, subg, lam_init):
    b, s, _ = k.shape
    tspec = pl.BlockSpec((1, LANES, s), lambda bi, hi: (bi, hi, 0))
    kspec = pl.BlockSpec((1, s, LANES), lambda bi, hi: (bi, 0, hi))
    acc = pltpu.VMEM((B_V_DIM + ATT_ONES_ROWS, s), F32)
    stat = pltpu.VMEM((1, s), F32)
    return pl.pallas_call(
        functools.partial(_attn_kernel, lam_init=lam_init),
        grid=(b, B_HEADS),
        in_specs=[_whole_vmem(), tspec, kspec, tspec, _whole_vmem()],
        out_specs=tspec,
        out_shape=jax.ShapeDtypeStruct((b, B_WIDTH, s), BF16),
        scratch_shapes=[acc, acc, stat, stat, pltpu.VMEM((2, LANES, s), BF16)],
        compiler_params=_params(2),
        name="attn",
    )(lamp, qt, k, vt, subg)


def _mixffn_kernel(x0_ref, ma0_ref, sgb0_ref, ybt0_ref, xn_ref, man_ref, sgbn_ref, ybtn_ref,
                   pb_ref, wo_ref, gain_ref, wg_ref, wu_ref, wd_ref, o_ref, a_ref, h_ref, g0_ref, x2_ref):
    def mix_a(ybt_ref):
        return _dot(ybt_ref[0].T, pb_ref[...])

    def mix_b(ma_ref, sgb_ref, ypb):
        m = ma_ref[...].astype(F32) + sgb_ref[...].astype(F32) * ypb
        return _dot(m.astype(BF16), wo_ref[...])

    def lead(x1_ref, mo):
        x2 = x1_ref[...] + mo
        x2_ref[...] = x2
        h = _rms_rows(x2, gain_ref[...], NORM_EPS).astype(BF16)
        h_ref[...] = h
        g0_ref[...] = _dot(h, wg_ref[:, :FFN_LEAD])

    @pl.when(pl.program_id(0) == 0)
    def _():
        lead(x0_ref, mix_b(ma0_ref, sgb0_ref, mix_a(ybt0_ref)))

    h = h_ref[...]
    a_ref[:, :FFN_LEAD] = (jax.nn.silu(g0_ref[...]) * _dot(h, wu_ref[:, :FFN_LEAD])).astype(BF16)
    for c0, c1 in ((FFN_LEAD, MIXFFN_F_SPLIT), (MIXFFN_F_SPLIT, D_FF)):
        g = _dot(h, wg_ref[:, c0:c1])
        u = _dot(h, wu_ref[:, c0:c1])
        a_ref[:, c0:c1] = (jax.nn.silu(g) * u).astype(BF16)
    half = D_MODEL // 2
    ypb = mix_a(ybtn_ref)
    y_lo = _dot(a_ref[...], wd_ref[:, :half])
    mo = mix_b(man_ref, sgbn_ref, ypb)
    y_hi = _dot(a_ref[...], wd_ref[:, half:])
    o_ref[:, :half] = x2_ref[:, :half] + 0.5 * y_lo
    o_ref[:, half:] = x2_ref[:, half:] + 0.5 * y_hi
    lead(xn_ref, mo)


def _mixffn(x1, ma, sgb, ybt, pb, wo, gain, wg, wu, wd):
    t = x1.shape[0]
    n = t // MIX_TM
    nsb = ybt.shape[2] // MIX_TM
    once = pl.Buffered(1)
    row0 = pl.BlockSpec((MIX_TM, D_MODEL), lambda i: (0, 0), pipeline_mode=once)
    col0 = pl.BlockSpec((1, B_WIDTH, MIX_TM), lambda i: (0, 0, 0), pipeline_mode=once)
    nxt = lambda i: jnp.minimum(i + 1, n - 1)
    rown = pl.BlockSpec((MIX_TM, D_MODEL), lambda i: (nxt(i), 0))
    coln = pl.BlockSpec((1, B_WIDTH, MIX_TM), lambda i: (nxt(i) // nsb, 0, nxt(i) % nsb))
    return pl.pallas_call(
        _mixffn_kernel,
        grid=(n,),
        in_specs=[row0, row0, row0, col0, rown, rown, rown, coln] + [_whole_vmem()] * 6,
        out_specs=pl.BlockSpec((MIX_TM, D_MODEL), lambda i: (i, 0)),
        out_shape=jax.ShapeDtypeStruct((t, D_MODEL), F32),
        scratch_shapes=[pltpu.VMEM((MIX_TM, D_FF), BF16), pltpu.VMEM((MIX_TM, D_MODEL), BF16),
                        pltpu.VMEM((MIX_TM, FFN_LEAD), F32), pltpu.VMEM((MIX_TM, D_MODEL), F32)],
        compiler_params=_params(1),
        name="mixffn",
    )(x1, ma, sgb, ybt, x1, ma, sgb, ybt, pb, wo, gain, wg, wu, wd)


def _rope_tables():
    half = ROT_DIM // 2
    freq = (ROPE_THETA ** (-jnp.arange(0, ROT_DIM, 2, dtype=F32) / ROT_DIM))[:, None]
    lane = jnp.arange(LANES) % B_QK_DIM
    hit = (lane[None, :] % half) == jnp.arange(half)[:, None]
    cos_m = jnp.where(hit & (lane < ROT_DIM)[None, :], 1.0, 0.0)
    slo_m = jnp.where(hit & (lane < half)[None, :], -1.0, 0.0)
    shi_m = jnp.where(hit & ((lane >= half) & (lane < ROT_DIM))[None, :], 1.0, 0.0)
    zero = jnp.zeros_like(cos_m)
    one_row = jnp.zeros((LANES - 4 * half, LANES), F32).at[0].set(jnp.where(lane >= ROT_DIM, 1.0, 0.0))
    rows = [jnp.concatenate([cos_m, zero, zero], axis=1)] * 2 + [jnp.concatenate([zero, slo_m, shi_m], axis=1)] * 2
    rows.append(jnp.concatenate([one_row, jnp.zeros_like(one_row), jnp.zeros_like(one_row)], axis=1))
    return freq, jnp.concatenate(rows, axis=0).astype(BF16)


def kernel(x, positions, ffn1_norm, ffn1_w_gate, ffn1_w_up, ffn1_w_down, mix_norm, w_in, a_ln_gain, a_ln_bias, a_w_s, a_b_s, a_w_proj, b_q_norm, b_k_norm, b_lambda_q1, b_lambda_k1, b_lambda_q2, b_lambda_k2, b_subln, b_w_proj, w_out, ffn2_norm, ffn2_w_gate, ffn2_w_up, ffn2_w_down):
    bsz, seq, _ = x.shape
    depth = w_in.shape[0]
    t = bsz * seq
    xt = x.reshape(t, D_MODEL)
    pos = positions.reshape(t // PROJ_TM, 1, PROJ_TM)
    freq, rope = _rope_tables()
    grp = jnp.arange(MXU_DIM) // B_QK_DIM
    gmat = jnp.where(grp[:, None] == grp[None, :], 1.0 / B_QK_DIM, 0.0).astype(BF16)

    for l in range(depth):
        lam_init = 0.8 - 0.6 * math.exp(-0.3 * l)
        row = lambda a: a[l][None, :].astype(F32)
        later = (w_in[l], a_w_proj[l], b_w_proj[l], w_out[l], ffn2_w_gate[l], ffn2_w_up[l], ffn2_w_down[l])
        xt, win_b, pa_b, pb_b, wo_b, wg2_b, wu2_b, wd2_b = _ffn(
            xt, row(ffn1_norm), ffn1_w_gate[l].astype(BF16), ffn1_w_up[l].astype(BF16),
            ffn1_w_down[l].astype(BF16), later)
        ws_pair = a_w_s[l].reshape(A_GROUPS // 2, 2, CHUNK, CHUNK).transpose(0, 2, 1, 3)
        ws_pair = ws_pair.reshape(A_GROUPS // 2, CHUNK, 2 * CHUNK).astype(BF16)
        bs_full = jnp.repeat(a_b_s[l].T, A_GROUP_DIM, axis=1).astype(F32)
        tile = lambda a, scale: jnp.tile(a[l].astype(F32) * scale, B_WIDTH // B_QK_DIM)[None, :]
        ma, sgb, qt, k, vt = _proj(
            xt, pos, row(mix_norm), win_b, row(a_ln_gain), row(a_ln_bias), ws_pair, bs_full,
            pa_b, gmat, tile(b_q_norm, B_QK_DIM ** -0.5 * LOG2E), tile(b_k_norm, 1.0), freq, rope, bsz)
        lamp = jnp.stack([b_lambda_q1[l], b_lambda_k1[l], b_lambda_q2[l], b_lambda_k2[l]]).astype(F32)
        subg = jnp.broadcast_to(b_subln[l].astype(F32)[:, None], (B_V_DIM, LANES))
        ybt = _attn(lamp, qt, k.reshape(bsz, seq, B_WIDTH), vt, subg, lam_init)
        xt = _mixffn(xt, ma, sgb, ybt, pb_b, wo_b, row(ffn2_norm), wg2_b, wu2_b, wd2_b)
    return xt.reshape(bsz, seq, D_MODEL)
```

```python
import functools
import math

import jax
import jax.numpy as jnp
from jax import lax
from jax.experimental import pallas as pl
from jax.experimental.pallas import tpu as pltpu

D_MODEL = 1024
D_FF = 2816
A_WIDTH = 512
A_GROUPS = 8
A_GROUP_DIM = A_WIDTH // A_GROUPS
CHUNK = 128
B_HEADS = 8
B_QK_DIM = 64
B_V_DIM = 2 * B_QK_DIM
B_WIDTH = B_HEADS * B_V_DIM
ROPE_THETA = 500000.0
ROT_DIM = B_QK_DIM // 4
NORM_EPS = 1e-6
LN_EPS = 1e-5

LANES = 128
BF16_SUBLANES = 16
MXU_DIM = 256
VMEM_LIMIT = 56 * 1024 * 1024

FFN_TM = 512
FFN_LEAD = MXU_DIM
MIXFFN_F_SPLIT = 6 * MXU_DIM
PROJ_TM = 512
MIX_TM = 512
ATT_CB = MXU_DIM
ATT_LOOKAHEAD = 5
ATT_ONES_ROWS = 16
LOG2E = math.log2(math.e)

BF16 = jnp.bfloat16
F32 = jnp.float32


def _dot(a, b):
    return jnp.dot(a, b, preferred_element_type=F32)


def _rms_rows(x, gain, eps):
    ms = jnp.mean(x * x, axis=-1, keepdims=True)
    return x * lax.rsqrt(ms + eps) * gain


def _whole_vmem():
    return pl.BlockSpec(memory_space=pltpu.VMEM)


def _params(n_axes):
    return pltpu.CompilerParams(
        dimension_semantics=("arbitrary",) * n_axes,
        vmem_limit_bytes=VMEM_LIMIT,
    )


def _ffn_kernel(x_ref, xn_ref, gain_ref, wg_ref, wu_ref, wd_ref, *rest):
    ncast = (len(rest) - 4) // 2
    srcs, o_ref, dsts = rest[:ncast], rest[ncast], rest[ncast + 1:2 * ncast + 1]
    a_ref, h_ref, g0_ref = rest[2 * ncast + 1:]
    for src, dst in zip(srcs, dsts):
        dst[...] = src[...].astype(BF16)

    def lead(src_ref):
        h = _rms_rows(src_ref[...], gain_ref[...], NORM_EPS).astype(BF16)
        h_ref[...] = h
        g0_ref[...] = _dot(h, wg_ref[:, :FFN_LEAD])

    @pl.when(pl.program_id(0) == 0)
    def _():
        lead(x_ref)

    h = h_ref[...]
    u = _dot(h, wu_ref[...])
    g = _dot(h, wg_ref[:, FFN_LEAD:])
    a_ref[:, :FFN_LEAD] = (jax.nn.silu(g0_ref[...]) * u[:, :FFN_LEAD]).astype(BF16)
    a_ref[:, FFN_LEAD:] = (jax.nn.silu(g) * u[:, FFN_LEAD:]).astype(BF16)
    y = _dot(a_ref[...], wd_ref[...])
    o_ref[...] = x_ref[...] + 0.5 * y
    lead(xn_ref)


def _slab_spec(w, n):
    rows = next(r for r in range(BF16_SUBLANES, w.shape[0] + 1, BF16_SUBLANES)
                if w.shape[0] % r == 0 and r * n >= w.shape[0])
    last = w.shape[0] // rows - 1
    return pl.BlockSpec((rows, w.shape[1]), lambda i: (jnp.minimum(i, last), 0))


def _ffn(x, gain, wg, wu, wd, later_weights):
    t = x.shape[0]
    n = t // FFN_TM
    row = pl.BlockSpec((FFN_TM, D_MODEL), lambda i: (i, 0))
    nxt = pl.BlockSpec((FFN_TM, D_MODEL), lambda i: (jnp.minimum(i + 1, n - 1), 0))
    slabs = [_slab_spec(w, n) for w in later_weights]
    return pl.pallas_call(
        _ffn_kernel,
        grid=(n,),
        in_specs=[row, nxt, _whole_vmem(), _whole_vmem(), _whole_vmem(), _whole_vmem()] + slabs,
        out_specs=[row] + slabs,
        out_shape=[jax.ShapeDtypeStruct((t, D_MODEL), F32)]
                  + [jax.ShapeDtypeStruct(w.shape, BF16) for w in later_weights],
        scratch_shapes=[pltpu.VMEM((FFN_TM, D_FF), BF16), pltpu.VMEM((FFN_TM, D_MODEL), BF16),
                        pltpu.VMEM((FFN_TM, FFN_LEAD), F32)],
        compiler_params=_params(1),
        name="ffn",
    )(x, x, gain, wg, wu, wd, *later_weights)


def _group_mean_sq(z, gmat_ref):
    zz = (z * z).astype(BF16)
    return jnp.concatenate(
        [_dot(zz[:, c:c + MXU_DIM], gmat_ref[...]) for c in range(0, B_WIDTH, MXU_DIM)], axis=1)


def _norm_rope(z, ms, gain, cos_t, sin_lo, sin_hi):
    zn = z * lax.rsqrt(ms + NORM_EPS) * gain
    half = ROT_DIM // 2
    up = pltpu.roll(zn, B_WIDTH - half, 1)
    dn = pltpu.roll(zn, half, 1)
    return (zn * cos_t + up * sin_lo + dn * sin_hi).astype(BF16)


def _proj_kernel(x_ref, pos_ref, gain_ref, win_ref, lng_ref, lnb_ref, ws_ref, bs_ref, pa_ref,
                 gmat_ref, qgt_ref, kg_ref, freq_ref, rope_ref,
                 ma_ref, sgb_ref, qt_ref, k_ref, vt_ref):
    tm = x_ref.shape[0]
    h = _rms_rows(x_ref[...], gain_ref[...], NORM_EPS).astype(BF16)

    def zcols(i):
        return _dot(h, win_ref[:, i * D_MODEL:(i + 1) * D_MODEL])

    zq = zcols(3)

    ang = freq_ref[...] * pos_ref[0].astype(F32)
    cos_r, sin_r = jnp.cos(ang), jnp.sin(ang)
    parts = []
    for tr in (cos_r, sin_r):
        hi = tr.astype(BF16).astype(F32)
        parts += [hi, tr - hi]
    parts.append(jnp.ones((LANES - 4 * ROT_DIM // 2, tm), F32))
    trig = jnp.concatenate(parts, axis=0).astype(BF16)
    tab = lax.dot_general(trig, rope_ref[...], (((0,), (0,)), ((), ())), preferred_element_type=F32)

    zk = zcols(4)
    zv = zcols(5)
    msk = _group_mean_sq(zk, gmat_ref)
    zuv = zcols(2)

    reps = B_WIDTH // LANES
    cos_t = jnp.concatenate([tab[:, :LANES]] * reps, axis=1)
    sin_lo = jnp.concatenate([tab[:, LANES:2 * LANES]] * reps, axis=1)
    sin_hi = jnp.concatenate([tab[:, 2 * LANES:]] * reps, axis=1)

    zqt = zq.T
    gain_t = jnp.concatenate([qgt_ref[...]] * (tm // LANES), axis=1)
    half = ROT_DIM // 2
    slabs = []
    for g0 in range(0, B_WIDTH, B_QK_DIM):
        zg = zqt[g0:g0 + B_QK_DIM]
        r = lax.rsqrt(jnp.mean(zg * zg, axis=0, keepdims=True) + NORM_EPS)
        zn = zg * gain_t[g0:g0 + B_QK_DIM] * r
        lo, hi = zn[:half], zn[half:ROT_DIM]
        slabs += [lo * cos_r - hi * sin_r, hi * cos_r + lo * sin_r, zn[ROT_DIM:]]
    qt_ref[0] = jnp.concatenate(slabs, axis=0).astype(BF16)

    zga = zcols(0)

    k_ref[...] = _norm_rope(zk, msk, kg_ref[...], cos_t, sin_lo, sin_hi)
    vt_ref[0] = zv.astype(BF16).T

    zuv = 0.5 * zuv * (1.0 + lax.erf(zuv * (2.0 ** -0.5)))
    u = zuv[:, :A_WIDTH]
    v = zuv[:, A_WIDTH:]
    mu = jnp.mean(v, axis=-1, keepdims=True)
    vc = v - mu
    var = jnp.mean(vc * vc, axis=-1, keepdims=True)
    vn = (vc * lax.rsqrt(var + LN_EPS) * lng_ref[...] + lnb_ref[...]).astype(BF16)

    zgb = zcols(1)

    lane = lax.broadcasted_iota(jnp.int32, (CHUNK, LANES), 1)
    lo_half = lane < A_GROUP_DIM
    row_t = lax.broadcasted_iota(jnp.int32, (CHUNK, 2 * CHUNK), 0)
    col_s = lax.broadcasted_iota(jnp.int32, (CHUNK, 2 * CHUNK), 1) % CHUNK
    causal = col_s <= row_t
    zero = jnp.zeros((), BF16)
    f_rows = []
    for c in range(tm // CHUNK):
        f_cols = []
        for j in range(A_WIDTH // LANES):
            vb = vn[c * CHUNK:(c + 1) * CHUNK, j * LANES:(j + 1) * LANES]
            rhs = jnp.concatenate([jnp.where(lo_half, vb, zero), jnp.where(lo_half, zero, vb)], axis=0)
            wpair = jnp.where(causal, ws_ref[j], zero)
            f_cols.append(_dot(wpair, rhs))
        f_rows.append(jnp.concatenate(f_cols, axis=1) + bs_ref[...])
    f = jnp.concatenate(f_rows, axis=0)

    ya = (u * f).astype(BF16)
    ma_ref[...] = (jax.nn.sigmoid(zga) * _dot(ya, pa_ref[...])).astype(BF16)
    sgb_ref[...] = jax.nn.sigmoid(zgb).astype(BF16)


def _proj(x1, pos, gain, win, lng, lnb, ws_pair, bs_full, pa, gmat, qg, kg, freq, rope, bsz):
    t = x1.shape[0]
    nsb = t // bsz // PROJ_TM
    row = pl.BlockSpec((PROJ_TM, D_MODEL), lambda i: (i, 0))
    col = pl.BlockSpec((1, B_WIDTH, PROJ_TM), lambda i: (i // nsb, 0, i % nsb))
    out = jax.ShapeDtypeStruct((t, D_MODEL), BF16)
    out_t = jax.ShapeDtypeStruct((bsz, B_WIDTH, t // bsz), BF16)
    return pl.pallas_call(
        _proj_kernel,
        grid=(t // PROJ_TM,),
        in_specs=[row, pl.BlockSpec((1, 1, PROJ_TM), lambda i: (i, 0, 0))] + [_whole_vmem()] * 12,
        out_specs=[row, row, col, row, col],
        out_shape=[out, out, out_t, out, out_t],
        compiler_params=_params(1),
        name="proj",
    )(x1, pos, gain, win, lng, lnb, ws_pair, bs_full, pa, gmat, qg, kg, freq, rope)


def _attn_kernel(lamp_ref, qt_ref, k_ref, vt_ref, subg_ref, ot_ref, acc1, acc2, m1, m2, qz_ref, *, lam_init):
    cb = ATT_CB
    seq = qt_ref.shape[2]
    nblk = seq // cb
    lo_rows = lax.broadcasted_iota(jnp.int32, (LANES, seq), 0) < B_QK_DIM
    zero = jnp.zeros((), BF16)
    qt = qt_ref[0]
    qz_ref[0] = jnp.where(lo_rows, qt, zero)
    qz_ref[1] = jnp.where(lo_rows, zero, qt)
    ones = jnp.ones((ATT_ONES_ROWS, cb), BF16)
    krow = lax.broadcasted_iota(jnp.int32, (cb, cb), 0)
    qcol = lax.broadcasted_iota(jnp.int32, (cb, cb), 1)
    diag = krow <= qcol
    stats = ((acc1, m1), (acc2, m2))

    def score(kblk, c, mi):
        return _dot(k_ref[0, kblk * cb:(kblk + 1) * cb, :], qz_ref[mi, :, c * cb:(c + 1) * cb])

    cells = [(kblk, c, mi) for kblk in range(nblk) for c in range(kblk, nblk) for mi in range(2)]
    pending = [score(*cell) for cell in cells[:ATT_LOOKAHEAD]]
    vte = None
    for i, (kblk, c, mi) in enumerate(cells):
        if i + ATT_LOOKAHEAD < len(cells):
            pending.append(score(*cells[i + ATT_LOOKAHEAD]))
        s = pending.pop(0)
        if c == kblk:
            s = jnp.where(diag, s, -jnp.inf)
            if mi == 0:
                vte = jnp.concatenate([vt_ref[0, :, kblk * cb:(kblk + 1) * cb], ones], axis=0)
        acc, m = stats[mi]
        cs = slice(c * cb, (c + 1) * cb)
        bmax = jnp.max(s, axis=0, keepdims=True)
        if kblk == 0:
            m_new = bmax
            acc[:, cs] = _dot(vte, jnp.exp2(s - m_new).astype(BF16))
        else:
            m_old = m[:, cs]
            m_new = jnp.maximum(m_old, bmax)
            alpha = jnp.exp2(m_old - m_new)
            acc[:, cs] = alpha * acc[:, cs] + _dot(vte, jnp.exp2(s - m_new).astype(BF16))
        m[:, cs] = m_new

    lp = lamp_ref[...]
    lam = (jnp.exp(jnp.sum(lp[0:1] * lp[1:2], axis=1, keepdims=True))
           - jnp.exp(jnp.sum(lp[2:3] * lp[3:4], axis=1, keepdims=True)) + lam_init)
    for c in range(nblk):
        cs = slice(c * cb, (c + 1) * cb)
        r1 = 1.0 / acc1[B_V_DIM:B_V_DIM + 1, cs]
        r2 = lam / acc2[B_V_DIM:B_V_DIM + 1, cs]
        ot = acc1[:B_V_DIM, cs] * r1 - acc2[:B_V_DIM, cs] * r2
        ms = jnp.mean(ot * ot, axis=0, keepdims=True)
        gain = jnp.concatenate([subg_ref[...]] * (cb // LANES), axis=1)
        ot_ref[0, :, cs] = (ot * (lax.rsqrt(ms + NORM_EPS) * (1.0 - lam_init)) * gain).astype(BF16)


def _attn(lamp, qt, k, vt, subg, lam_init):
    b, s, _ = k.shape
    tspec = pl.BlockSpec((1, LANES, s), lambda bi, hi: (bi, hi, 0))
    kspec = pl.BlockSpec((1, s, LANES), lambda bi, hi: (bi, 0, hi))
    acc = pltpu.VMEM((B_V_DIM + ATT_ONES_ROWS, s), F32)
    stat = pltpu.VMEM((1, s), F32)
    return pl.pallas_call(
        functools.partial(_attn_kernel, lam_init=lam_init),
        grid=(b, B_HEADS),
        in_specs=[_whole_vmem(), tspec, kspec, tspec, _whole_vmem()],
        out_specs=tspec,
        out_shape=jax.ShapeDtypeStruct((b, B_WIDTH, s), BF16),
        scratch_shapes=[acc, acc, stat, stat, pltpu.VMEM((2, LANES, s), BF16)],
        compiler_params=_params(2),
        name="attn",
    )(lamp, qt, k, vt, subg)


def _mixffn_kernel(x0_ref, ma0_ref, sgb0_ref, ybt0_ref, xn_ref, man_ref, sgbn_ref, ybtn_ref,
                   pb_ref, wo_ref, gain_ref, wg_ref, wu_ref, wd_ref, o_ref, a_ref, h_ref, g0_ref, x2_ref):
    def mix_a(ybt_ref):
        return _dot(ybt_ref[0].T, pb_ref[...])

    def mix_b(ma_ref, sgb_ref, ypb):
        m = ma_ref[...].astype(F32) + sgb_ref[...].astype(F32) * ypb
        return _dot(m.astype(BF16), wo_ref[...])

    def lead(x1_ref, mo):
        x2 = x1_ref[...] + mo
        x2_ref[...] = x2
        h = _rms_rows(x2, gain_ref[...], NORM_EPS).astype(BF16)
        h_ref[...] = h
        g0_ref[...] = _dot(h, wg_ref[:, :FFN_LEAD])

    @pl.when(pl.program_id(0) == 0)
    def _():
        lead(x0_ref, mix_b(ma0_ref, sgb0_ref, mix_a(ybt0_ref)))

    h = h_ref[...]
    a_ref[:, :FFN_LEAD] = (jax.nn.silu(g0_ref[...]) * _dot(h, wu_ref[:, :FFN_LEAD])).astype(BF16)
    for c0, c1 in ((FFN_LEAD, MIXFFN_F_SPLIT), (MIXFFN_F_SPLIT, D_FF)):
        g = _dot(h, wg_ref[:, c0:c1])
        u = _dot(h, wu_ref[:, c0:c1])
        a_ref[:, c0:c1] = (jax.nn.silu(g) * u).astype(BF16)
    half = D_MODEL // 2
    ypb = mix_a(ybtn_ref)
    y_lo = _dot(a_ref[...], wd_ref[:, :half])
    mo = mix_b(man_ref, sgbn_ref, ypb)
    y_hi = _dot(a_ref[...], wd_ref[:, half:])
    o_ref[:, :half] = x2_ref[:, :half] + 0.5 * y_lo
    o_ref[:, half:] = x2_ref[:, half:] + 0.5 * y_hi
    lead(xn_ref, mo)


def _mixffn(x1, ma, sgb, ybt, pb, wo, gain, wg, wu, wd):
    t = x1.shape[0]
    n = t // MIX_TM
    nsb = ybt.shape[2] // MIX_TM
    once = pl.Buffered(1)
    row0 = pl.BlockSpec((MIX_TM, D_MODEL), lambda i: (0, 0), pipeline_mode=once)
    col0 = pl.BlockSpec((1, B_WIDTH, MIX_TM), lambda i: (0, 0, 0), pipeline_mode=once)
    nxt = lambda i: jnp.minimum(i + 1, n - 1)
    rown = pl.BlockSpec((MIX_TM, D_MODEL), lambda i: (nxt(i), 0))
    coln = pl.BlockSpec((1, B_WIDTH, MIX_TM), lambda i: (nxt(i) // nsb, 0, nxt(i) % nsb))
    return pl.pallas_call(
        _mixffn_kernel,
        grid=(n,),
        in_specs=[row0, row0, row0, col0, rown, rown, rown, coln] + [_whole_vmem()] * 6,
        out_specs=pl.BlockSpec((MIX_TM, D_MODEL), lambda i: (i, 0)),
        out_shape=jax.ShapeDtypeStruct((t, D_MODEL), F32),
        scratch_shapes=[pltpu.VMEM((MIX_TM, D_FF), BF16), pltpu.VMEM((MIX_TM, D_MODEL), BF16),
                        pltpu.VMEM((MIX_TM, FFN_LEAD), F32), pltpu.VMEM((MIX_TM, D_MODEL), F32)],
        compiler_params=_params(1),
        name="mixffn",
    )(x1, ma, sgb, ybt, x1, ma, sgb, ybt, pb, wo, gain, wg, wu, wd)


def _rope_tables():
    half = ROT_DIM // 2
    freq = (ROPE_THETA ** (-jnp.arange(0, ROT_DIM, 2, dtype=F32) / ROT_DIM))[:, None]
    lane = jnp.arange(LANES) % B_QK_DIM
    hit = (lane[None, :] % half) == jnp.arange(half)[:, None]
    cos_m = jnp.where(hit & (lane < ROT_DIM)[None, :], 1.0, 0.0)
    slo_m = jnp.where(hit & (lane < half)[None, :], -1.0, 0.0)
    shi_m = jnp.where(hit & ((lane >= half) & (lane < ROT_DIM))[None, :], 1.0, 0.0)
    zero = jnp.zeros_like(cos_m)
    one_row = jnp.zeros((LANES - 4 * half, LANES), F32).at[0].set(jnp.where(lane >= ROT_DIM, 1.0, 0.0))
    rows = [jnp.concatenate([cos_m, zero, zero], axis=1)] * 2 + [jnp.concatenate([zero, slo_m, shi_m], axis=1)] * 2
    rows.append(jnp.concatenate([one_row, jnp.zeros_like(one_row), jnp.zeros_like(one_row)], axis=1))
    return freq, jnp.concatenate(rows, axis=0).astype(BF16)


def kernel(x, positions, ffn1_norm, ffn1_w_gate, ffn1_w_up, ffn1_w_down, mix_norm, w_in, a_ln_gain, a_ln_bias, a_w_s, a_b_s, a_w_proj, b_q_norm, b_k_norm, b_lambda_q1, b_lambda_k1, b_lambda_q2, b_lambda_k2, b_subln, b_w_proj, w_out, ffn2_norm, ffn2_w_gate, ffn2_w_up, ffn2_w_down):
    bsz, seq, _ = x.shape
    depth = w_in.shape[0]
    t = bsz * seq
    xt = x.reshape(t, D_MODEL)
    pos = positions.reshape(t // PROJ_TM, 1, PROJ_TM)
    freq, rope = _rope_tables()
    grp = jnp.arange(MXU_DIM) // B_QK_DIM
    gmat = jnp.where(grp[:, None] == grp[None, :], 1.0 / B_QK_DIM, 0.0).astype(BF16)

    for l in range(depth):
        lam_init = 0.8 - 0.6 * math.exp(-0.3 * l)
        row = lambda a: a[l][None, :].astype(F32)
        later = (w_in[l], a_w_proj[l], b_w_proj[l], w_out[l], ffn2_w_gate[l], ffn2_w_up[l], ffn2_w_down[l])
        xt, win_b, pa_b, pb_b, wo_b, wg2_b, wu2_b, wd2_b = _ffn(
            xt, row(ffn1_norm), ffn1_w_gate[l].astype(BF16), ffn1_w_up[l].astype(BF16),
            ffn1_w_down[l].astype(BF16), later)
        ws_pair = a_w_s[l].reshape(A_GROUPS // 2, 2, CHUNK, CHUNK).transpose(0, 2, 1, 3)
        ws_pair = ws_pair.reshape(A_GROUPS // 2, CHUNK, 2 * CHUNK).astype(BF16)
        bs_full = jnp.repeat(a_b_s[l].T, A_GROUP_DIM, axis=1).astype(F32)
        tile = lambda a, scale: jnp.tile(a[l].astype(F32) * scale, B_WIDTH // B_QK_DIM)[None, :]
        qgt = jnp.broadcast_to(tile(b_q_norm, B_QK_DIM ** -0.5 * LOG2E).T, (B_WIDTH, LANES))
        ma, sgb, qt, k, vt = _proj(
            xt, pos, row(mix_norm), win_b, row(a_ln_gain), row(a_ln_bias), ws_pair, bs_full,
            pa_b, gmat, qgt, tile(b_k_norm, 1.0), freq, rope, bsz)
        lamp = jnp.stack([b_lambda_q1[l], b_lambda_k1[l], b_lambda_q2[l], b_lambda_k2[l]]).astype(F32)
        subg = jnp.broadcast_to(b_subln[l].astype(F32)[:, None], (B_V_DIM, LANES))
        ybt = _attn(lamp, qt, k.reshape(bsz, seq, B_WIDTH), vt, subg, lam_init)
        xt = _mixffn(xt, ma, sgb, ybt, pb_b, wo_b, row(ffn2_norm), wg2_b, wu2_b, wd2_b)
    return xt.reshape(bsz, seq, D_MODEL)
```

```python
import functools
import math

import jax
import jax.numpy as jnp
from jax import lax
from jax.experimental import pallas as pl
from jax.experimental.pallas import tpu as pltpu

D_MODEL = 1024
D_FF = 2816
A_WIDTH = 512
A_GROUPS = 8
A_GROUP_DIM = A_WIDTH // A_GROUPS
CHUNK = 128
B_HEADS = 8
B_QK_DIM = 64
B_V_DIM = 2 * B_QK_DIM
B_WIDTH = B_HEADS * B_V_DIM
ROPE_THETA = 500000.0
ROT_DIM = B_QK_DIM // 4
NORM_EPS = 1e-6
LN_EPS = 1e-5

LANES = 128
BF16_SUBLANES = 16
MXU_DIM = 256
VMEM_LIMIT = 56 * 1024 * 1024

FFN_TM = 512
FFN_LEAD = MXU_DIM
MIXFFN_F_SPLIT = 6 * MXU_DIM
PROJ_TM = 512
MIX_TM = 512
ATT_CB = MXU_DIM
ATT_LOOKAHEAD = 5
ATT_ONES_ROWS = 16
LOG2E = math.log2(math.e)

BF16 = jnp.bfloat16
F32 = jnp.float32


def _dot(a, b):
    return jnp.dot(a, b, preferred_element_type=F32)


def _rms_rows(x, gain, eps):
    ms = jnp.mean(x * x, axis=-1, keepdims=True)
    return x * lax.rsqrt(ms + eps) * gain


def _whole_vmem():
    return pl.BlockSpec(memory_space=pltpu.VMEM)


def _params(n_axes):
    return pltpu.CompilerParams(
        dimension_semantics=("arbitrary",) * n_axes,
        vmem_limit_bytes=VMEM_LIMIT,
    )


def _ffn_kernel(x_ref, xn_ref, gain_ref, wg_ref, wu_ref, wd_ref, *rest):
    ncast = (len(rest) - 4) // 2
    srcs, o_ref, dsts = rest[:ncast], rest[ncast], rest[ncast + 1:2 * ncast + 1]
    a_ref, h_ref, g0_ref = rest[2 * ncast + 1:]
    for src, dst in zip(srcs, dsts):
        dst[...] = src[...].astype(BF16)

    def lead(src_ref):
        h = _rms_rows(src_ref[...], gain_ref[...], NORM_EPS).astype(BF16)
        h_ref[...] = h
        g0_ref[...] = _dot(h, wg_ref[:, :FFN_LEAD])

    @pl.when(pl.program_id(0) == 0)
    def _():
        lead(x_ref)

    h = h_ref[...]
    u = _dot(h, wu_ref[...])
    g = _dot(h, wg_ref[:, FFN_LEAD:])
    a_ref[:, :FFN_LEAD] = (jax.nn.silu(g0_ref[...]) * u[:, :FFN_LEAD]).astype(BF16)
    a_ref[:, FFN_LEAD:] = (jax.nn.silu(g) * u[:, FFN_LEAD:]).astype(BF16)
    y = _dot(a_ref[...], wd_ref[...])
    o_ref[...] = x_ref[...] + 0.5 * y
    lead(xn_ref)


def _slab_spec(w, n):
    rows = next(r for r in range(BF16_SUBLANES, w.shape[0] + 1, BF16_SUBLANES)
                if w.shape[0] % r == 0 and r * n >= w.shape[0])
    last = w.shape[0] // rows - 1
    return pl.BlockSpec((rows, w.shape[1]), lambda i: (jnp.minimum(i, last), 0))


def _ffn(x, gain, wg, wu, wd, later_weights):
    t = x.shape[0]
    n = t // FFN_TM
    row = pl.BlockSpec((FFN_TM, D_MODEL), lambda i: (i, 0))
    nxt = pl.BlockSpec((FFN_TM, D_MODEL), lambda i: (jnp.minimum(i + 1, n - 1), 0))
    slabs = [_slab_spec(w, n) for w in later_weights]
    return pl.pallas_call(
        _ffn_kernel,
        grid=(n,),
        in_specs=[row, nxt, _whole_vmem(), _whole_vmem(), _whole_vmem(), _whole_vmem()] + slabs,
        out_specs=[row] + slabs,
        out_shape=[jax.ShapeDtypeStruct((t, D_MODEL), F32)]
                  + [jax.ShapeDtypeStruct(w.shape, BF16) for w in later_weights],
        scratch_shapes=[pltpu.VMEM((FFN_TM, D_FF), BF16), pltpu.VMEM((FFN_TM, D_MODEL), BF16),
                        pltpu.VMEM((FFN_TM, FFN_LEAD), F32)],
        compiler_params=_params(1),
        name="ffn",
    )(x, x, gain, wg, wu, wd, *later_weights)


def _norm_rope_t(zt, gain_ref, cos_r, sin_r):
    gain_t = jnp.concatenate([gain_ref[...]] * (zt.shape[1] // LANES), axis=1)
    half = ROT_DIM // 2
    slabs = []
    for g0 in range(0, B_WIDTH, B_QK_DIM):
        zg = zt[g0:g0 + B_QK_DIM]
        r = lax.rsqrt(jnp.mean(zg * zg, axis=0, keepdims=True) + NORM_EPS)
        zn = zg * gain_t[g0:g0 + B_QK_DIM] * r
        lo, hi = zn[:half], zn[half:ROT_DIM]
        slabs += [lo * cos_r - hi * sin_r, hi * cos_r + lo * sin_r, zn[ROT_DIM:]]
    return jnp.concatenate(slabs, axis=0).astype(BF16)


def _proj_kernel(x_ref, pos_ref, gain_ref, win_ref, lng_ref, lnb_ref, ws_ref, bs_ref, pa_ref,
                 qgt_ref, kgt_ref, freq_ref,
                 ma_ref, sgb_ref, qt_ref, k_ref, vt_ref):
    tm = x_ref.shape[0]
    h = _rms_rows(x_ref[...], gain_ref[...], NORM_EPS).astype(BF16)

    def zcols(i):
        return _dot(h, win_ref[:, i * D_MODEL:(i + 1) * D_MODEL])

    zq = zcols(3)

    ang = freq_ref[...] * pos_ref[0].astype(F32)
    cos_r, sin_r = jnp.cos(ang), jnp.sin(ang)

    zk = zcols(4)
    zv = zcols(5)
    zuv = zcols(2)

    qt_ref[0] = _norm_rope_t(zq.T, qgt_ref, cos_r, sin_r)

    zga = zcols(0)

    k_ref[...] = _norm_rope_t(zk.T, kgt_ref, cos_r, sin_r).T
    vt_ref[0] = zv.astype(BF16).T

    zuv = 0.5 * zuv * (1.0 + lax.erf(zuv * (2.0 ** -0.5)))
    u = zuv[:, :A_WIDTH]
    v = zuv[:, A_WIDTH:]
    mu = jnp.mean(v, axis=-1, keepdims=True)
    vc = v - mu
    var = jnp.mean(vc * vc, axis=-1, keepdims=True)
    vn = (vc * lax.rsqrt(var + LN_EPS) * lng_ref[...] + lnb_ref[...]).astype(BF16)

    zgb = zcols(1)

    lane = lax.broadcasted_iota(jnp.int32, (CHUNK, LANES), 1)
    lo_half = lane < A_GROUP_DIM
    row_t = lax.broadcasted_iota(jnp.int32, (CHUNK, 2 * CHUNK), 0)
    col_s = lax.broadcasted_iota(jnp.int32, (CHUNK, 2 * CHUNK), 1) % CHUNK
    causal = col_s <= row_t
    zero = jnp.zeros((), BF16)
    f_rows = []
    for c in range(tm // CHUNK):
        f_cols = []
        for j in range(A_WIDTH // LANES):
            vb = vn[c * CHUNK:(c + 1) * CHUNK, j * LANES:(j + 1) * LANES]
            rhs = jnp.concatenate([jnp.where(lo_half, vb, zero), jnp.where(lo_half, zero, vb)], axis=0)
            wpair = jnp.where(causal, ws_ref[j], zero)
            f_cols.append(_dot(wpair, rhs))
        f_rows.append(jnp.concatenate(f_cols, axis=1) + bs_ref[...])
    f = jnp.concatenate(f_rows, axis=0)

    ya = (u * f).astype(BF16)
    ma_ref[...] = (jax.nn.sigmoid(zga) * _dot(ya, pa_ref[...])).astype(BF16)
    sgb_ref[...] = jax.nn.sigmoid(zgb).astype(BF16)


def _proj(x1, pos, gain, win, lng, lnb, ws_pair, bs_full, pa, qgt, kgt, freq, bsz):
    t = x1.shape[0]
    nsb = t // bsz // PROJ_TM
    row = pl.BlockSpec((PROJ_TM, D_MODEL), lambda i: (i, 0))
    col = pl.BlockSpec((1, B_WIDTH, PROJ_TM), lambda i: (i // nsb, 0, i % nsb))
    out = jax.ShapeDtypeStruct((t, D_MODEL), BF16)
    out_t = jax.ShapeDtypeStruct((bsz, B_WIDTH, t // bsz), BF16)
    return pl.pallas_call(
        _proj_kernel,
        grid=(t // PROJ_TM,),
        in_specs=[row, pl.BlockSpec((1, 1, PROJ_TM), lambda i: (i, 0, 0))] + [_whole_vmem()] * 10,
        out_specs=[row, row, col, row, col],
        out_shape=[out, out, out_t, out, out_t],
        compiler_params=_params(1),
        name="proj",
    )(x1, pos, gain, win, lng, lnb, ws_pair, bs_full, pa, qgt, kgt, freq)


def _attn_kernel(lamp_ref, qt_ref, k_ref, vt_ref, subg_ref, ot_ref, acc1, acc2, m1, m2, qz_ref, *, lam_init):
    cb = ATT_CB
    seq = qt_ref.shape[2]
    nblk = seq // cb
    lo_rows = lax.broadcasted_iota(jnp.int32, (LANES, seq), 0) < B_QK_DIM
    zero = jnp.zeros((), BF16)
    qt = qt_ref[0]
    qz_ref[0] = jnp.where(lo_rows, qt, zero)
    qz_ref[1] = jnp.where(lo_rows, zero, qt)
    ones = jnp.ones((ATT_ONES_ROWS, cb), BF16)
    krow = lax.broadcasted_iota(jnp.int32, (cb, cb), 0)
    qcol = lax.broadcasted_iota(jnp.int32, (cb, cb), 1)
    diag = krow <= qcol
    stats = ((acc1, m1), (acc2, m2))

    def score(kblk, c, mi):
        return _dot(k_ref[0, kblk * cb:(kblk + 1) * cb, :], qz_ref[mi, :, c * cb:(c + 1) * cb])

    cells = [(kblk, c, mi) for kblk in range(nblk) for c in range(kblk, nblk) for mi in range(2)]
    pending = [score(*cell) for cell in cells[:ATT_LOOKAHEAD]]
    vte = None
    for i, (kblk, c, mi) in enumerate(cells):
        if i + ATT_LOOKAHEAD < len(cells):
            pending.append(score(*cells[i + ATT_LOOKAHEAD]))
        s = pending.pop(0)
        if c == kblk:
            s = jnp.where(diag, s, -jnp.inf)
            if mi == 0:
                vte = jnp.concatenate([vt_ref[0, :, kblk * cb:(kblk + 1) * cb], ones], axis=0)
        acc, m = stats[mi]
        cs = slice(c * cb, (c + 1) * cb)
        bmax = jnp.max(s, axis=0, keepdims=True)
        if kblk == 0:
            m_new = bmax
            acc[:, cs] = _dot(vte, jnp.exp2(s - m_new).astype(BF16))
        else:
            m_old = m[:, cs]
            m_new = jnp.maximum(m_old, bmax)
            alpha = jnp.exp2(m_old - m_new)
            acc[:, cs] = alpha * acc[:, cs] + _dot(vte, jnp.exp2(s - m_new).astype(BF16))
        m[:, cs] = m_new

    lp = lamp_ref[...]
    lam = (jnp.exp(jnp.sum(lp[0:1] * lp[1:2], axis=1, keepdims=True))
           - jnp.exp(jnp.sum(lp[2:3] * lp[3:4], axis=1, keepdims=True)) + lam_init)
    for c in range(nblk):
        cs = slice(c * cb, (c + 1) * cb)
        r1 = 1.0 / acc1[B_V_DIM:B_V_DIM + 1, cs]
        r2 = lam / acc2[B_V_DIM:B_V_DIM + 1, cs]
        ot = acc1[:B_V_DIM, cs] * r1 - acc2[:B_V_DIM, cs] * r2
        ms = jnp.mean(ot * ot, axis=0, keepdims=True)
        gain = jnp.concatenate([subg_ref[...]] * (cb // LANES), axis=1)
        ot_ref[0, :, cs] = (ot * (lax.rsqrt(ms + NORM_EPS) * (1.0 - lam_init)) * gain).astype(BF16)


def _attn(lamp, qt, k, vt, subg, lam_init):
    b, s, _ = k.shape
    tspec = pl.BlockSpec((1, LANES, s), lambda bi, hi: (bi, hi, 0))
    kspec = pl.BlockSpec((1, s, LANES), lambda bi, hi: (bi, 0, hi))
    acc = pltpu.VMEM((B_V_DIM + ATT_ONES_ROWS, s), F32)
    stat = pltpu.VMEM((1, s), F32)
    return pl.pallas_call(
        functools.partial(_attn_kernel, lam_init=lam_init),
        grid=(b, B_HEADS),
        in_specs=[_whole_vmem(), tspec, kspec, tspec, _whole_vmem()],
        out_specs=tspec,
        out_shape=jax.ShapeDtypeStruct((b, B_WIDTH, s), BF16),
        scratch_shapes=[acc, acc, stat, stat, pltpu.VMEM((2, LANES, s), BF16)],
        compiler_params=_params(2),
        name="attn",
    )(lamp, qt, k, vt, subg)


def _mixffn_kernel(x0_ref, ma0_ref, sgb0_ref, ybt0_ref, xn_ref, man_ref, sgbn_ref, ybtn_ref,
                   pb_ref, wo_ref, gain_ref, wg_ref, wu_ref, wd_ref, o_ref, a_ref, h_ref, g0_ref, x2_ref):
    def mix_a(ybt_ref):
        return _dot(ybt_ref[0].T, pb_ref[...])

    def mix_b(ma_ref, sgb_ref, ypb):
        m = ma_ref[...].astype(F32) + sgb_ref[...].astype(F32) * ypb
        return _dot(m.astype(BF16), wo_ref[...])

    def lead(x1_ref, mo):
        x2 = x1_ref[...] + mo
        x2_ref[...] = x2
        h = _rms_rows(x2, gain_ref[...], NORM_EPS).astype(BF16)
        h_ref[...] = h
        g0_ref[...] = _dot(h, wg_ref[:, :FFN_LEAD])

    @pl.when(pl.program_id(0) == 0)
    def _():
        lead(x0_ref, mix_b(ma0_ref, sgb0_ref, mix_a(ybt0_ref)))

    h = h_ref[...]
    a_ref[:, :FFN_LEAD] = (jax.nn.silu(g0_ref[...]) * _dot(h, wu_ref[:, :FFN_LEAD])).astype(BF16)
    for c0, c1 in ((FFN_LEAD, MIXFFN_F_SPLIT), (MIXFFN_F_SPLIT, D_FF)):
        g = _dot(h, wg_ref[:, c0:c1])
        u = _dot(h, wu_ref[:, c0:c1])
        a_ref[:, c0:c1] = (jax.nn.silu(g) * u).astype(BF16)
    half = D_MODEL // 2
    ypb = mix_a(ybtn_ref)
    y_lo = _dot(a_ref[...], wd_ref[:, :half])
    mo = mix_b(man_ref, sgbn_ref, ypb)
    y_hi = _dot(a_ref[...], wd_ref[:, half:])
    o_ref[:, :half] = x2_ref[:, :half] + 0.5 * y_lo
    o_ref[:, half:] = x2_ref[:, half:] + 0.5 * y_hi
    lead(xn_ref, mo)


def _mixffn(x1, ma, sgb, ybt, pb, wo, gain, wg, wu, wd):
    t = x1.shape[0]
    n = t // MIX_TM
    nsb = ybt.shape[2] // MIX_TM
    once = pl.Buffered(1)
    row0 = pl.BlockSpec((MIX_TM, D_MODEL), lambda i: (0, 0), pipeline_mode=once)
    col0 = pl.BlockSpec((1, B_WIDTH, MIX_TM), lambda i: (0, 0, 0), pipeline_mode=once)
    nxt = lambda i: jnp.minimum(i + 1, n - 1)
    rown = pl.BlockSpec((MIX_TM, D_MODEL), lambda i: (nxt(i), 0))
    coln = pl.BlockSpec((1, B_WIDTH, MIX_TM), lambda i: (nxt(i) // nsb, 0, nxt(i) % nsb))
    return pl.pallas_call(
        _mixffn_kernel,
        grid=(n,),
        in_specs=[row0, row0, row0, col0, rown, rown, rown, coln] + [_whole_vmem()] * 6,
        out_specs=pl.BlockSpec((MIX_TM, D_MODEL), lambda i: (i, 0)),
        out_shape=jax.ShapeDtypeStruct((t, D_MODEL), F32),
        scratch_shapes=[pltpu.VMEM((MIX_TM, D_FF), BF16), pltpu.VMEM((MIX_TM, D_MODEL), BF16),
                        pltpu.VMEM((MIX_TM, FFN_LEAD), F32), pltpu.VMEM((MIX_TM, D_MODEL), F32)],
        compiler_params=_params(1),
        name="mixffn",
    )(x1, ma, sgb, ybt, x1, ma, sgb, ybt, pb, wo, gain, wg, wu, wd)


def kernel(x, positions, ffn1_norm, ffn1_w_gate, ffn1_w_up, ffn1_w_down, mix_norm, w_in, a_ln_gain, a_ln_bias, a_w_s, a_b_s, a_w_proj, b_q_norm, b_k_norm, b_lambda_q1, b_lambda_k1, b_lambda_q2, b_lambda_k2, b_subln, b_w_proj, w_out, ffn2_norm, ffn2_w_gate, ffn2_w_up, ffn2_w_down):
    bsz, seq, _ = x.shape
    depth = w_in.shape[0]
    t = bsz * seq
    xt = x.reshape(t, D_MODEL)
    pos = positions.reshape(t // PROJ_TM, 1, PROJ_TM)
    freq = (ROPE_THETA ** (-jnp.arange(0, ROT_DIM, 2, dtype=F32) / ROT_DIM))[:, None]

    for l in range(depth):
        lam_init = 0.8 - 0.6 * math.exp(-0.3 * l)
        row = lambda a: a[l][None, :].astype(F32)
        later = (w_in[l], a_w_proj[l], b_w_proj[l], w_out[l], ffn2_w_gate[l], ffn2_w_up[l], ffn2_w_down[l])
        xt, win_b, pa_b, pb_b, wo_b, wg2_b, wu2_b, wd2_b = _ffn(
            xt, row(ffn1_norm), ffn1_w_gate[l].astype(BF16), ffn1_w_up[l].astype(BF16),
            ffn1_w_down[l].astype(BF16), later)
        ws_pair = a_w_s[l].reshape(A_GROUPS // 2, 2, CHUNK, CHUNK).transpose(0, 2, 1, 3)
        ws_pair = ws_pair.reshape(A_GROUPS // 2, CHUNK, 2 * CHUNK).astype(BF16)
        bs_full = jnp.repeat(a_b_s[l].T, A_GROUP_DIM, axis=1).astype(F32)
        gain_rows = lambda a, scale: jnp.broadcast_to(
            jnp.tile(a[l].astype(F32) * scale, B_WIDTH // B_QK_DIM)[:, None], (B_WIDTH, LANES))
        ma, sgb, qt, k, vt = _proj(
            xt, pos, row(mix_norm), win_b, row(a_ln_gain), row(a_ln_bias), ws_pair, bs_full,
            pa_b, gain_rows(b_q_norm, B_QK_DIM ** -0.5 * LOG2E), gain_rows(b_k_norm, 1.0), freq, bsz)
        lamp = jnp.stack([b_lambda_q1[l], b_lambda_k1[l], b_lambda_q2[l], b_lambda_k2[l]]).astype(F32)
        subg = jnp.broadcast_to(b_subln[l].astype(F32)[:, None], (B_V_DIM, LANES))
        ybt = _attn(lamp, qt, k.reshape(bsz, seq, B_WIDTH), vt, subg, lam_init)
        xt = _mixffn(xt, ma, sgb, ybt, pb_b, wo_b, row(ffn2_norm), wg2_b, wu2_b, wd2_b)
    return xt.reshape(bsz, seq, D_MODEL)
```

```python
import functools
import math

import jax
import jax.numpy as jnp
from jax import lax
from jax.experimental import pallas as pl
from jax.experimental.pallas import tpu as pltpu

D_MODEL = 1024
D_FF = 2816
A_WIDTH = 512
A_GROUPS = 8
A_GROUP_DIM = A_WIDTH // A_GROUPS
CHUNK = 128
B_HEADS = 8
B_QK_DIM = 64
B_V_DIM = 2 * B_QK_DIM
B_WIDTH = B_HEADS * B_V_DIM
ROPE_THETA = 500000.0
ROT_DIM = B_QK_DIM // 4
NORM_EPS = 1e-6
LN_EPS = 1e-5

LANES = 128
BF16_SUBLANES = 16
MXU_DIM = 256
VMEM_LIMIT = 56 * 1024 * 1024

FFN_TM = 512
FFN_LEAD = MXU_DIM
MIXFFN_F_SPLIT = 6 * MXU_DIM
PROJ_TM = 512
MIX_TM = 512
ATT_CB = MXU_DIM
ATT_LOOKAHEAD = 5
ATT_ONES_ROWS = 16
LOG2E = math.log2(math.e)

BF16 = jnp.bfloat16
F32 = jnp.float32


def _dot(a, b):
    return jnp.dot(a, b, preferred_element_type=F32)


def _rms_rows(x, gain, eps):
    ms = jnp.mean(x * x, axis=-1, keepdims=True)
    return x * lax.rsqrt(ms + eps) * gain


def _whole_vmem():
    return pl.BlockSpec(memory_space=pltpu.VMEM)


def _params(n_axes):
    return pltpu.CompilerParams(
        dimension_semantics=("arbitrary",) * n_axes,
        vmem_limit_bytes=VMEM_LIMIT,
    )


def _ffn_kernel(x_ref, xn_ref, gain_ref, wg_ref, wu_ref, wd_ref, *rest):
    ncast = (len(rest) - 4) // 2
    srcs, o_ref, dsts = rest[:ncast], rest[ncast], rest[ncast + 1:2 * ncast + 1]
    a_ref, h_ref, g0_ref = rest[2 * ncast + 1:]
    for src, dst in zip(srcs, dsts):
        dst[...] = src[...].astype(BF16)

    def lead(src_ref):
        h = _rms_rows(src_ref[...], gain_ref[...], NORM_EPS).astype(BF16)
        h_ref[...] = h
        g0_ref[...] = _dot(h, wg_ref[:, :FFN_LEAD])

    @pl.when(pl.program_id(0) == 0)
    def _():
        lead(x_ref)

    h = h_ref[...]
    u = _dot(h, wu_ref[...])
    g = _dot(h, wg_ref[:, FFN_LEAD:])
    a_ref[:, :FFN_LEAD] = (jax.nn.silu(g0_ref[...]) * u[:, :FFN_LEAD]).astype(BF16)
    a_ref[:, FFN_LEAD:] = (jax.nn.silu(g) * u[:, FFN_LEAD:]).astype(BF16)
    y = _dot(a_ref[...], wd_ref[...])
    o_ref[...] = x_ref[...] + 0.5 * y
    lead(xn_ref)


def _slab_spec(w, n):
    rows = next(r for r in range(BF16_SUBLANES, w.shape[0] + 1, BF16_SUBLANES)
                if w.shape[0] % r == 0 and r * n >= w.shape[0])
    last = w.shape[0] // rows - 1
    return pl.BlockSpec((rows, w.shape[1]), lambda i: (jnp.minimum(i, last), 0))


def _ffn(x, gain, wg, wu, wd, later_weights):
    t = x.shape[0]
    n = t // FFN_TM
    row = pl.BlockSpec((FFN_TM, D_MODEL), lambda i: (i, 0))
    nxt = pl.BlockSpec((FFN_TM, D_MODEL), lambda i: (jnp.minimum(i + 1, n - 1), 0))
    slabs = [_slab_spec(w, n) for w in later_weights]
    return pl.pallas_call(
        _ffn_kernel,
        grid=(n,),
        in_specs=[row, nxt, _whole_vmem(), _whole_vmem(), _whole_vmem(), _whole_vmem()] + slabs,
        out_specs=[row] + slabs,
        out_shape=[jax.ShapeDtypeStruct((t, D_MODEL), F32)]
                  + [jax.ShapeDtypeStruct(w.shape, BF16) for w in later_weights],
        scratch_shapes=[pltpu.VMEM((FFN_TM, D_FF), BF16), pltpu.VMEM((FFN_TM, D_MODEL), BF16),
                        pltpu.VMEM((FFN_TM, FFN_LEAD), F32)],
        compiler_params=_params(1),
        name="ffn",
    )(x, x, gain, wg, wu, wd, *later_weights)


def _norm_rope_t(zt, gain_ref, cos_r, sin_r):
    gain_t = jnp.concatenate([gain_ref[...]] * (zt.shape[1] // LANES), axis=1)
    half = ROT_DIM // 2
    slabs = []
    for g0 in range(0, B_WIDTH, B_QK_DIM):
        zg = zt[g0:g0 + B_QK_DIM]
        r = lax.rsqrt(jnp.mean(zg * zg, axis=0, keepdims=True) + NORM_EPS)
        zn = zg * gain_t[g0:g0 + B_QK_DIM] * r
        lo, hi = zn[:half], zn[half:ROT_DIM]
        slabs += [lo * cos_r - hi * sin_r, hi * cos_r + lo * sin_r, zn[ROT_DIM:]]
    return jnp.concatenate(slabs, axis=0).astype(BF16)


def _proj_kernel(x_ref, pos_ref, gain_ref, win_ref, lng_ref, lnb_ref, ws_ref, bs_ref, pa_ref,
                 qgt_ref, kgt_ref, freq_ref,
                 ma_ref, sgb_ref, qt_ref, k_ref, vt_ref):
    tm = x_ref.shape[0]
    h = _rms_rows(x_ref[...], gain_ref[...], NORM_EPS).astype(BF16)

    def zcols(i):
        return _dot(h, win_ref[:, i * D_MODEL:(i + 1) * D_MODEL])

    zq = zcols(3)

    ang = freq_ref[...] * pos_ref[0].astype(F32)
    cos_r, sin_r = jnp.cos(ang), jnp.sin(ang)

    zk = zcols(4)
    zv = zcols(5)
    zuv = zcols(2)

    qt_ref[0] = _norm_rope_t(zq.T, qgt_ref, cos_r, sin_r)

    zga = zcols(0)

    k_ref[...] = _norm_rope_t(zk.T, kgt_ref, cos_r, sin_r).T
    vt_ref[0] = zv.astype(BF16).T

    zuv = 0.5 * zuv * (1.0 + lax.erf(zuv * (2.0 ** -0.5)))
    u = zuv[:, :A_WIDTH]
    v = zuv[:, A_WIDTH:]
    mu = jnp.mean(v, axis=-1, keepdims=True)
    vc = v - mu
    var = jnp.mean(vc * vc, axis=-1, keepdims=True)
    vn = (vc * lax.rsqrt(var + LN_EPS) * lng_ref[...] + lnb_ref[...]).astype(BF16)

    zgb = zcols(1)

    lane = lax.broadcasted_iota(jnp.int32, (CHUNK, LANES), 1)
    lo_half = lane < A_GROUP_DIM
    row_t = lax.broadcasted_iota(jnp.int32, (CHUNK, 2 * CHUNK), 0)
    col_s = lax.broadcasted_iota(jnp.int32, (CHUNK, 2 * CHUNK), 1) % CHUNK
    causal = col_s <= row_t
    zero = jnp.zeros((), BF16)
    f_rows = []
    for c in range(tm // CHUNK):
        f_cols = []
        for j in range(A_WIDTH // LANES):
            vb = vn[c * CHUNK:(c + 1) * CHUNK, j * LANES:(j + 1) * LANES]
            rhs = jnp.concatenate([jnp.where(lo_half, vb, zero), jnp.where(lo_half, zero, vb)], axis=0)
            wpair = jnp.where(causal, ws_ref[j], zero)
            f_cols.append(_dot(wpair, rhs))
        f_rows.append(jnp.concatenate(f_cols, axis=1) + bs_ref[...])
    f = jnp.concatenate(f_rows, axis=0)

    ya = (u * f).astype(BF16)
    ma_ref[...] = (jax.nn.sigmoid(zga) * _dot(ya, pa_ref[...])).astype(BF16)
    sgb_ref[...] = jax.nn.sigmoid(zgb).astype(BF16)


def _proj(x1, pos, gain, win, lng, lnb, ws_pair, bs_full, pa, qgt, kgt, freq, bsz):
    t = x1.shape[0]
    nsb = t // bsz // PROJ_TM
    row = pl.BlockSpec((PROJ_TM, D_MODEL), lambda i: (i, 0))
    col = pl.BlockSpec((1, B_WIDTH, PROJ_TM), lambda i: (i // nsb, 0, i % nsb))
    out = jax.ShapeDtypeStruct((t, D_MODEL), BF16)
    out_t = jax.ShapeDtypeStruct((bsz, B_WIDTH, t // bsz), BF16)
    return pl.pallas_call(
        _proj_kernel,
        grid=(t // PROJ_TM,),
        in_specs=[row, pl.BlockSpec((1, 1, PROJ_TM), lambda i: (i, 0, 0))] + [_whole_vmem()] * 10,
        out_specs=[row, row, col, row, col],
        out_shape=[out, out, out_t, out, out_t],
        compiler_params=_params(1),
        name="proj",
    )(x1, pos, gain, win, lng, lnb, ws_pair, bs_full, pa, qgt, kgt, freq)


def _attn_kernel(lamp_ref, qt_ref, k_ref, vt_ref, subg_ref, ot_ref, acc1, acc2, m1, m2, qz_ref, *, lam_init):
    cb = ATT_CB
    seq = qt_ref.shape[2]
    nblk = seq // cb
    lo_rows = lax.broadcasted_iota(jnp.int32, (LANES, seq), 0) < B_QK_DIM
    zero = jnp.zeros((), BF16)
    qt = qt_ref[0]
    qz_ref[0] = jnp.where(lo_rows, qt, zero)
    qz_ref[1] = jnp.where(lo_rows, zero, qt)
    ones = jnp.ones((ATT_ONES_ROWS, cb), BF16)
    krow = lax.broadcasted_iota(jnp.int32, (cb, cb), 0)
    qcol = lax.broadcasted_iota(jnp.int32, (cb, cb), 1)
    diag = krow <= qcol
    stats = ((acc1, m1), (acc2, m2))

    def score(kblk, c, mi):
        return _dot(k_ref[0, kblk * cb:(kblk + 1) * cb, :], qz_ref[mi, :, c * cb:(c + 1) * cb])

    cells = [(kblk, c, mi) for kblk in range(nblk) for c in range(kblk, nblk) for mi in range(2)]
    pending = [score(*cell) for cell in cells[:ATT_LOOKAHEAD]]
    vte = None
    for i, (kblk, c, mi) in enumerate(cells):
        if i + ATT_LOOKAHEAD < len(cells):
            pending.append(score(*cells[i + ATT_LOOKAHEAD]))
        s = pending.pop(0).astype(BF16)
        if c == kblk:
            s = jnp.where(diag, s, jnp.asarray(-jnp.inf, BF16))
            if mi == 0:
                vte = jnp.concatenate([vt_ref[0, :, kblk * cb:(kblk + 1) * cb], ones], axis=0)
        acc, m = stats[mi]
        cs = slice(c * cb, (c + 1) * cb)
        bmax = jnp.max(s, axis=0, keepdims=True).astype(F32)
        if kblk == 0:
            m_new = bmax
            acc[:, cs] = _dot(vte, jnp.exp2(s - m_new.astype(BF16)))
        else:
            m_old = m[:, cs]
            m_new = jnp.maximum(m_old, bmax)
            alpha = jnp.exp2(m_old - m_new)
            acc[:, cs] = alpha * acc[:, cs] + _dot(vte, jnp.exp2(s - m_new.astype(BF16)))
        m[:, cs] = m_new

    lp = lamp_ref[...]
    lam = (jnp.exp(jnp.sum(lp[0:1] * lp[1:2], axis=1, keepdims=True))
           - jnp.exp(jnp.sum(lp[2:3] * lp[3:4], axis=1, keepdims=True)) + lam_init)
    for c in range(nblk):
        cs = slice(c * cb, (c + 1) * cb)
        r1 = 1.0 / acc1[B_V_DIM:B_V_DIM + 1, cs]
        r2 = lam / acc2[B_V_DIM:B_V_DIM + 1, cs]
        ot = acc1[:B_V_DIM, cs] * r1 - acc2[:B_V_DIM, cs] * r2
        ms = jnp.mean(ot * ot, axis=0, keepdims=True)
        gain = jnp.concatenate([subg_ref[...]] * (cb // LANES), axis=1)
        ot_ref[0, :, cs] = (ot * (lax.rsqrt(ms + NORM_EPS) * (1.0 - lam_init)) * gain).astype(BF16)


def _attn(lamp, qt, k, vt, subg, lam_init):
    b, s, _ = k.shape
    tspec = pl.BlockSpec((1, LANES, s), lambda bi, hi: (bi, hi, 0))
    kspec = pl.BlockSpec((1, s, LANES), lambda bi, hi: (bi, 0, hi))
    acc = pltpu.VMEM((B_V_DIM + ATT_ONES_ROWS, s), F32)
    stat = pltpu.VMEM((1, s), F32)
    return pl.pallas_call(
        functools.partial(_attn_kernel, lam_init=lam_init),
        grid=(b, B_HEADS),
        in_specs=[_whole_vmem(), tspec, kspec, tspec, _whole_vmem()],
        out_specs=tspec,
        out_shape=jax.ShapeDtypeStruct((b, B_WIDTH, s), BF16),
        scratch_shapes=[acc, acc, stat, stat, pltpu.VMEM((2, LANES, s), BF16)],
        compiler_params=_params(2),
        name="attn",
    )(lamp, qt, k, vt, subg)


def _mixffn_kernel(x0_ref, ma0_ref, sgb0_ref, ybt0_ref, xn_ref, man_ref, sgbn_ref, ybtn_ref,
                   pb_ref, wo_ref, gain_ref, wg_ref, wu_ref, wd_ref, o_ref, a_ref, h_ref, g0_ref, x2_ref):
    def mix_a(ybt_ref):
        return _dot(ybt_ref[0].T, pb_ref[...])

    def mix_b(ma_ref, sgb_ref, ypb):
        m = ma_ref[...].astype(F32) + sgb_ref[...].astype(F32) * ypb
        return _dot(m.astype(BF16), wo_ref[...])

    def lead(x1_ref, mo):
        x2 = x1_ref[...] + mo
        x2_ref[...] = x2
        h = _rms_rows(x2, gain_ref[...], NORM_EPS).astype(BF16)
        h_ref[...] = h
        g0_ref[...] = _dot(h, wg_ref[:, :FFN_LEAD])

    @pl.when(pl.program_id(0) == 0)
    def _():
        lead(x0_ref, mix_b(ma0_ref, sgb0_ref, mix_a(ybt0_ref)))

    h = h_ref[...]
    a_ref[:, :FFN_LEAD] = (jax.nn.silu(g0_ref[...]) * _dot(h, wu_ref[:, :FFN_LEAD])).astype(BF16)
    for c0, c1 in ((FFN_LEAD, MIXFFN_F_SPLIT), (MIXFFN_F_SPLIT, D_FF)):
        g = _dot(h, wg_ref[:, c0:c1])
        u = _dot(h, wu_ref[:, c0:c1])
        a_ref[:, c0:c1] = (jax.nn.silu(g) * u).astype(BF16)
    half = D_MODEL // 2
    ypb = mix_a(ybtn_ref)
    y_lo = _dot(a_ref[...], wd_ref[:, :half])
    mo = mix_b(man_ref, sgbn_ref, ypb)
    y_hi = _dot(a_ref[...], wd_ref[:, half:])
    o_ref[:, :half] = x2_ref[:, :half] + 0.5 * y_lo
    o_ref[:, half:] = x2_ref[:, half:] + 0.5 * y_hi
    lead(xn_ref, mo)


def _mixffn(x1, ma, sgb, ybt, pb, wo, gain, wg, wu, wd):
    t = x1.shape[0]
    n = t // MIX_TM
    nsb = ybt.shape[2] // MIX_TM
    once = pl.Buffered(1)
    row0 = pl.BlockSpec((MIX_TM, D_MODEL), lambda i: (0, 0), pipeline_mode=once)
    col0 = pl.BlockSpec((1, B_WIDTH, MIX_TM), lambda i: (0, 0, 0), pipeline_mode=once)
    nxt = lambda i: jnp.minimum(i + 1, n - 1)
    rown = pl.BlockSpec((MIX_TM, D_MODEL), lambda i: (nxt(i), 0))
    coln = pl.BlockSpec((1, B_WIDTH, MIX_TM), lambda i: (nxt(i) // nsb, 0, nxt(i) % nsb))
    return pl.pallas_call(
        _mixffn_kernel,
        grid=(n,),
        in_specs=[row0, row0, row0, col0, rown, rown, rown, coln] + [_whole_vmem()] * 6,
        out_specs=pl.BlockSpec((MIX_TM, D_MODEL), lambda i: (i, 0)),
        out_shape=jax.ShapeDtypeStruct((t, D_MODEL), F32),
        scratch_shapes=[pltpu.VMEM((MIX_TM, D_FF), BF16), pltpu.VMEM((MIX_TM, D_MODEL), BF16),
                        pltpu.VMEM((MIX_TM, FFN_LEAD), F32), pltpu.VMEM((MIX_TM, D_MODEL), F32)],
        compiler_params=_params(1),
        name="mixffn",
    )(x1, ma, sgb, ybt, x1, ma, sgb, ybt, pb, wo, gain, wg, wu, wd)


def kernel(x, positions, ffn1_norm, ffn1_w_gate, ffn1_w_up, ffn1_w_down, mix_norm, w_in, a_ln_gain, a_ln_bias, a_w_s, a_b_s, a_w_proj, b_q_norm, b_k_norm, b_lambda_q1, b_lambda_k1, b_lambda_q2, b_lambda_k2, b_subln, b_w_proj, w_out, ffn2_norm, ffn2_w_gate, ffn2_w_up, ffn2_w_down):
    bsz, seq, _ = x.shape
    depth = w_in.shape[0]
    t = bsz * seq
    xt = x.reshape(t, D_MODEL)
    pos = positions.reshape(t // PROJ_TM, 1, PROJ_TM)
    freq = (ROPE_THETA ** (-jnp.arange(0, ROT_DIM, 2, dtype=F32) / ROT_DIM))[:, None]

    for l in range(depth):
        lam_init = 0.8 - 0.6 * math.exp(-0.3 * l)
        row = lambda a: a[l][None, :].astype(F32)
        later = (w_in[l], a_w_proj[l], b_w_proj[l], w_out[l], ffn2_w_gate[l], ffn2_w_up[l], ffn2_w_down[l])
        xt, win_b, pa_b, pb_b, wo_b, wg2_b, wu2_b, wd2_b = _ffn(
            xt, row(ffn1_norm), ffn1_w_gate[l].astype(BF16), ffn1_w_up[l].astype(BF16),
            ffn1_w_down[l].astype(BF16), later)
        ws_pair = a_w_s[l].reshape(A_GROUPS // 2, 2, CHUNK, CHUNK).transpose(0, 2, 1, 3)
        ws_pair = ws_pair.reshape(A_GROUPS // 2, CHUNK, 2 * CHUNK).astype(BF16)
        bs_full = jnp.repeat(a_b_s[l].T, A_GROUP_DIM, axis=1).astype(F32)
        gain_rows = lambda a, scale: jnp.broadcast_to(
            jnp.tile(a[l].astype(F32) * scale, B_WIDTH // B_QK_DIM)[:, None], (B_WIDTH, LANES))
        ma, sgb, qt, k, vt = _proj(
            xt, pos, row(mix_norm), win_b, row(a_ln_gain), row(a_ln_bias), ws_pair, bs_full,
            pa_b, gain_rows(b_q_norm, B_QK_DIM ** -0.5 * LOG2E), gain_rows(b_k_norm, 1.0), freq, bsz)
        lamp = jnp.stack([b_lambda_q1[l], b_lambda_k1[l], b_lambda_q2[l], b_lambda_k2[l]]).astype(F32)
        subg = jnp.broadcast_to(b_subln[l].astype(F32)[:, None], (B_V_DIM, LANES))
        ybt = _attn(lamp, qt, k.reshape(bsz, seq, B_WIDTH), vt, subg, lam_init)
        xt = _mixffn(xt, ma, sgb, ybt, pb_b, wo_b, row(ffn2_norm), wg2_b, wu2_b, wd2_b)
    return xt.reshape(bsz, seq, D_MODEL)
```

```python
import functools
import math

import jax
import jax.numpy as jnp
from jax import lax
from jax.experimental import pallas as pl
from jax.experimental.pallas import tpu as pltpu

D_MODEL = 1024
D_FF = 2816
A_WIDTH = 512
A_GROUPS = 8
A_GROUP_DIM = A_WIDTH // A_GROUPS
CHUNK = 128
B_HEADS = 8
B_QK_DIM = 64
B_V_DIM = 2 * B_QK_DIM
B_WIDTH = B_HEADS * B_V_DIM
ROPE_THETA = 500000.0
ROT_DIM = B_QK_DIM // 4
NORM_EPS = 1e-6
LN_EPS = 1e-5

LANES = 128
BF16_SUBLANES = 16
MXU_DIM = 256
VMEM_LIMIT = 56 * 1024 * 1024

FFN_TM = 512
FFN_LEAD = MXU_DIM
MIXFFN_F_SPLIT = 6 * MXU_DIM
PROJ_TM = 512
MIX_TM = 512
ATT_CB = MXU_DIM
ATT_HEADS_PER_STEP = 2
ATT_LOOKAHEAD = 5
ATT_ONES_ROWS = 16
LOG2E = math.log2(math.e)

BF16 = jnp.bfloat16
F32 = jnp.float32


def _dot(a, b):
    return jnp.dot(a, b, preferred_element_type=F32)


def _rms_rows(x, gain, eps):
    ms = jnp.mean(x * x, axis=-1, keepdims=True)
    return x * lax.rsqrt(ms + eps) * gain


def _whole_vmem():
    return pl.BlockSpec(memory_space=pltpu.VMEM)


def _params(n_axes):
    return pltpu.CompilerParams(
        dimension_semantics=("arbitrary",) * n_axes,
        vmem_limit_bytes=VMEM_LIMIT,
    )


def _ffn_kernel(x_ref, xn_ref, gain_ref, wg_ref, wu_ref, wd_ref, *rest):
    ncast = (len(rest) - 4) // 2
    srcs, o_ref, dsts = rest[:ncast], rest[ncast], rest[ncast + 1:2 * ncast + 1]
    a_ref, h_ref, g0_ref = rest[2 * ncast + 1:]
    for src, dst in zip(srcs, dsts):
        dst[...] = src[...].astype(BF16)

    def lead(src_ref):
        h = _rms_rows(src_ref[...], gain_ref[...], NORM_EPS).astype(BF16)
        h_ref[...] = h
        g0_ref[...] = _dot(h, wg_ref[:, :FFN_LEAD])

    @pl.when(pl.program_id(0) == 0)
    def _():
        lead(x_ref)

    h = h_ref[...]
    u = _dot(h, wu_ref[...])
    g = _dot(h, wg_ref[:, FFN_LEAD:])
    a_ref[:, :FFN_LEAD] = (jax.nn.silu(g0_ref[...]) * u[:, :FFN_LEAD]).astype(BF16)
    a_ref[:, FFN_LEAD:] = (jax.nn.silu(g) * u[:, FFN_LEAD:]).astype(BF16)
    y = _dot(a_ref[...], wd_ref[...])
    o_ref[...] = x_ref[...] + 0.5 * y
    lead(xn_ref)


def _slab_spec(w, n):
    rows = next(r for r in range(BF16_SUBLANES, w.shape[0] + 1, BF16_SUBLANES)
                if w.shape[0] % r == 0 and r * n >= w.shape[0])
    last = w.shape[0] // rows - 1
    return pl.BlockSpec((rows, w.shape[1]), lambda i: (jnp.minimum(i, last), 0))


def _ffn(x, gain, wg, wu, wd, later_weights):
    t = x.shape[0]
    n = t // FFN_TM
    row = pl.BlockSpec((FFN_TM, D_MODEL), lambda i: (i, 0))
    nxt = pl.BlockSpec((FFN_TM, D_MODEL), lambda i: (jnp.minimum(i + 1, n - 1), 0))
    slabs = [_slab_spec(w, n) for w in later_weights]
    return pl.pallas_call(
        _ffn_kernel,
        grid=(n,),
        in_specs=[row, nxt, _whole_vmem(), _whole_vmem(), _whole_vmem(), _whole_vmem()] + slabs,
        out_specs=[row] + slabs,
        out_shape=[jax.ShapeDtypeStruct((t, D_MODEL), F32)]
                  + [jax.ShapeDtypeStruct(w.shape, BF16) for w in later_weights],
        scratch_shapes=[pltpu.VMEM((FFN_TM, D_FF), BF16), pltpu.VMEM((FFN_TM, D_MODEL), BF16),
                        pltpu.VMEM((FFN_TM, FFN_LEAD), F32)],
        compiler_params=_params(1),
        name="ffn",
    )(x, x, gain, wg, wu, wd, *later_weights)


def _norm_rope_t(zt, gain_ref, cos_r, sin_r):
    gain_t = jnp.concatenate([gain_ref[...]] * (zt.shape[1] // LANES), axis=1)
    half = ROT_DIM // 2
    slabs = []
    for g0 in range(0, B_WIDTH, B_QK_DIM):
        zg = zt[g0:g0 + B_QK_DIM]
        r = lax.rsqrt(jnp.mean(zg * zg, axis=0, keepdims=True) + NORM_EPS)
        zn = zg * gain_t[g0:g0 + B_QK_DIM] * r
        lo, hi = zn[:half], zn[half:ROT_DIM]
        slabs += [lo * cos_r - hi * sin_r, hi * cos_r + lo * sin_r, zn[ROT_DIM:]]
    return jnp.concatenate(slabs, axis=0).astype(BF16)


def _proj_kernel(x_ref, pos_ref, gain_ref, win_ref, lng_ref, lnb_ref, ws_ref, bs_ref, pa_ref,
                 qgt_ref, kgt_ref, freq_ref,
                 ma_ref, sgb_ref, qt_ref, k_ref, vt_ref):
    tm = x_ref.shape[0]
    h = _rms_rows(x_ref[...], gain_ref[...], NORM_EPS).astype(BF16)

    def zcols(i):
        return _dot(h, win_ref[:, i * D_MODEL:(i + 1) * D_MODEL])

    zq = zcols(3)

    ang = freq_ref[...] * pos_ref[0].astype(F32)
    cos_r, sin_r = jnp.cos(ang), jnp.sin(ang)

    zk = zcols(4)
    zv = zcols(5)
    zuv = zcols(2)

    qt_ref[0] = _norm_rope_t(zq.T, qgt_ref, cos_r, sin_r)

    zga = zcols(0)

    k_ref[...] = _norm_rope_t(zk.T, kgt_ref, cos_r, sin_r).T
    vt_ref[0] = zv.astype(BF16).T

    zuv = 0.5 * zuv * (1.0 + lax.erf(zuv * (2.0 ** -0.5)))
    u = zuv[:, :A_WIDTH]
    v = zuv[:, A_WIDTH:]
    mu = jnp.mean(v, axis=-1, keepdims=True)
    vc = v - mu
    var = jnp.mean(vc * vc, axis=-1, keepdims=True)
    vn = (vc * lax.rsqrt(var + LN_EPS) * lng_ref[...] + lnb_ref[...]).astype(BF16)

    zgb = zcols(1)

    lane = lax.broadcasted_iota(jnp.int32, (CHUNK, LANES), 1)
    lo_half = lane < A_GROUP_DIM
    row_t = lax.broadcasted_iota(jnp.int32, (CHUNK, 2 * CHUNK), 0)
    col_s = lax.broadcasted_iota(jnp.int32, (CHUNK, 2 * CHUNK), 1) % CHUNK
    causal = col_s <= row_t
    zero = jnp.zeros((), BF16)
    f_rows = []
    for c in range(tm // CHUNK):
        f_cols = []
        for j in range(A_WIDTH // LANES):
            vb = vn[c * CHUNK:(c + 1) * CHUNK, j * LANES:(j + 1) * LANES]
            rhs = jnp.concatenate([jnp.where(lo_half, vb, zero), jnp.where(lo_half, zero, vb)], axis=0)
            wpair = jnp.where(causal, ws_ref[j], zero)
            f_cols.append(_dot(wpair, rhs))
        f_rows.append(jnp.concatenate(f_cols, axis=1) + bs_ref[...])
    f = jnp.concatenate(f_rows, axis=0)

    ya = (u * f).astype(BF16)
    ma_ref[...] = (jax.nn.sigmoid(zga) * _dot(ya, pa_ref[...])).astype(BF16)
    sgb_ref[...] = jax.nn.sigmoid(zgb).astype(BF16)


def _proj(x1, pos, gain, win, lng, lnb, ws_pair, bs_full, pa, qgt, kgt, freq, bsz):
    t = x1.shape[0]
    nsb = t // bsz // PROJ_TM
    row = pl.BlockSpec((PROJ_TM, D_MODEL), lambda i: (i, 0))
    col = pl.BlockSpec((1, B_WIDTH, PROJ_TM), lambda i: (i // nsb, 0, i % nsb))
    out = jax.ShapeDtypeStruct((t, D_MODEL), BF16)
    out_t = jax.ShapeDtypeStruct((bsz, B_WIDTH, t // bsz), BF16)
    return pl.pallas_call(
        _proj_kernel,
        grid=(t // PROJ_TM,),
        in_specs=[row, pl.BlockSpec((1, 1, PROJ_TM), lambda i: (i, 0, 0))] + [_whole_vmem()] * 10,
        out_specs=[row, row, col, row, col],
        out_shape=[out, out, out_t, out, out_t],
        compiler_params=_params(1),
        name="proj",
    )(x1, pos, gain, win, lng, lnb, ws_pair, bs_full, pa, qgt, kgt, freq)


def _attn_head(lamp_ref, qt_ref, k_ref, vt_ref, subg_ref, ot_ref, acc1, acc2, m1, m2, qz_ref, *, lam_init):
    cb = ATT_CB
    seq = qt_ref.shape[2]
    nblk = seq // cb
    lo_rows = lax.broadcasted_iota(jnp.int32, (LANES, seq), 0) < B_QK_DIM
    zero = jnp.zeros((), BF16)
    qt = qt_ref[0]
    qz_ref[0] = jnp.where(lo_rows, qt, zero)
    qz_ref[1] = jnp.where(lo_rows, zero, qt)
    ones = jnp.ones((ATT_ONES_ROWS, cb), BF16)
    krow = lax.broadcasted_iota(jnp.int32, (cb, cb), 0)
    qcol = lax.broadcasted_iota(jnp.int32, (cb, cb), 1)
    diag = krow <= qcol
    stats = ((acc1, m1), (acc2, m2))

    def score(kblk, c, mi):
        return _dot(k_ref[0, kblk * cb:(kblk + 1) * cb, :], qz_ref[mi, :, c * cb:(c + 1) * cb])

    cells = [(kblk, c, mi) for kblk in range(nblk) for c in range(kblk, nblk) for mi in range(2)]
    pending = [score(*cell) for cell in cells[:ATT_LOOKAHEAD]]
    vte = None
    for i, (kblk, c, mi) in enumerate(cells):
        if i + ATT_LOOKAHEAD < len(cells):
            pending.append(score(*cells[i + ATT_LOOKAHEAD]))
        s = pending.pop(0).astype(BF16)
        if c == kblk:
            s = jnp.where(diag, s, jnp.asarray(-jnp.inf, BF16))
            if mi == 0:
                vte = jnp.concatenate([vt_ref[0, :, kblk * cb:(kblk + 1) * cb], ones], axis=0)
        acc, m = stats[mi]
        cs = slice(c * cb, (c + 1) * cb)
        bmax = jnp.max(s, axis=0, keepdims=True).astype(F32)
        if kblk == 0:
            m_new = bmax
            acc[:, cs] = _dot(vte, jnp.exp2(s - m_new.astype(BF16)))
        else:
            m_old = m[:, cs]
            m_new = jnp.maximum(m_old, bmax)
            alpha = jnp.exp2(m_old - m_new)
            acc[:, cs] = alpha * acc[:, cs] + _dot(vte, jnp.exp2(s - m_new.astype(BF16)))
        m[:, cs] = m_new

    lp = lamp_ref[...]
    lam = (jnp.exp(jnp.sum(lp[0:1] * lp[1:2], axis=1, keepdims=True))
           - jnp.exp(jnp.sum(lp[2:3] * lp[3:4], axis=1, keepdims=True)) + lam_init)
    for c in range(nblk):
        cs = slice(c * cb, (c + 1) * cb)
        r1 = 1.0 / acc1[B_V_DIM:B_V_DIM + 1, cs]
        r2 = lam / acc2[B_V_DIM:B_V_DIM + 1, cs]
        ot = acc1[:B_V_DIM, cs] * r1 - acc2[:B_V_DIM, cs] * r2
        ms = jnp.mean(ot * ot, axis=0, keepdims=True)
        gain = jnp.concatenate([subg_ref[...]] * (cb // LANES), axis=1)
        ot_ref[0, :, cs] = (ot * (lax.rsqrt(ms + NORM_EPS) * (1.0 - lam_init)) * gain).astype(BF16)


def _attn_kernel(lamp_ref, qt_ref, k_ref, vt_ref, subg_ref, ot_ref, acc1, acc2, m1, m2, qz_ref, *, lam_init):
    for hp in range(ATT_HEADS_PER_STEP):
        hs = slice(hp * LANES, (hp + 1) * LANES)
        _attn_head(lamp_ref, qt_ref.at[:, hs, :], k_ref.at[:, :, hs], vt_ref.at[:, hs, :], subg_ref, ot_ref.at[:, hs, :],
                   acc1.at[hp], acc2.at[hp], m1.at[hp], m2.at[hp], qz_ref.at[2 * hp:2 * hp + 2], lam_init=lam_init)


def _attn(lamp, qt, k, vt, subg, lam_init):
    b, s, _ = k.shape
    hw = ATT_HEADS_PER_STEP * LANES
    tspec = pl.BlockSpec((1, hw, s), lambda bi, hi: (bi, hi, 0))
    kspec = pl.BlockSpec((1, s, hw), lambda bi, hi: (bi, 0, hi))
    acc = pltpu.VMEM((ATT_HEADS_PER_STEP, B_V_DIM + ATT_ONES_ROWS, s), F32)
    stat = pltpu.VMEM((ATT_HEADS_PER_STEP, 1, s), F32)
    return pl.pallas_call(
        functools.partial(_attn_kernel, lam_init=lam_init),
        grid=(b, B_HEADS // ATT_HEADS_PER_STEP),
        in_specs=[_whole_vmem(), tspec, kspec, tspec, _whole_vmem()],
        out_specs=tspec,
        out_shape=jax.ShapeDtypeStruct((b, B_WIDTH, s), BF16),
        scratch_shapes=[acc, acc, stat, stat, pltpu.VMEM((2 * ATT_HEADS_PER_STEP, LANES, s), BF16)],
        compiler_params=_params(2),
        name="attn",
    )(lamp, qt, k, vt, subg)


def _mixffn_kernel(x0_ref, ma0_ref, sgb0_ref, ybt0_ref, xn_ref, man_ref, sgbn_ref, ybtn_ref,
                   pb_ref, wo_ref, gain_ref, wg_ref, wu_ref, wd_ref, o_ref, a_ref, h_ref, g0_ref, x2_ref):
    def mix_a(ybt_ref):
        return _dot(ybt_ref[0].T, pb_ref[...])

    def mix_b(ma_ref, sgb_ref, ypb):
        m = ma_ref[...].astype(F32) + sgb_ref[...].astype(F32) * ypb
        return _dot(m.astype(BF16), wo_ref[...])

    def lead(x1_ref, mo):
        x2 = x1_ref[...] + mo
        x2_ref[...] = x2
        h = _rms_rows(x2, gain_ref[...], NORM_EPS).astype(BF16)
        h_ref[...] = h
        g0_ref[...] = _dot(h, wg_ref[:, :FFN_LEAD])

    @pl.when(pl.program_id(0) == 0)
    def _():
        lead(x0_ref, mix_b(ma0_ref, sgb0_ref, mix_a(ybt0_ref)))

    h = h_ref[...]
    a_ref[:, :FFN_LEAD] = (jax.nn.silu(g0_ref[...]) * _dot(h, wu_ref[:, :FFN_LEAD])).astype(BF16)
    for c0, c1 in ((FFN_LEAD, MIXFFN_F_SPLIT), (MIXFFN_F_SPLIT, D_FF)):
        g = _dot(h, wg_ref[:, c0:c1])
        u = _dot(h, wu_ref[:, c0:c1])
        a_ref[:, c0:c1] = (jax.nn.silu(g) * u).astype(BF16)
    half = D_MODEL // 2
    ypb = mix_a(ybtn_ref)
    y_lo = _dot(a_ref[...], wd_ref[:, :half])
    mo = mix_b(man_ref, sgbn_ref, ypb)
    y_hi = _dot(a_ref[...], wd_ref[:, half:])
    o_ref[:, :half] = x2_ref[:, :half] + 0.5 * y_lo
    o_ref[:, half:] = x2_ref[:, half:] + 0.5 * y_hi
    lead(xn_ref, mo)


def _mixffn(x1, ma, sgb, ybt, pb, wo, gain, wg, wu, wd):
    t = x1.shape[0]
    n = t // MIX_TM
    nsb = ybt.shape[2] // MIX_TM
    once = pl.Buffered(1)
    row0 = pl.BlockSpec((MIX_TM, D_MODEL), lambda i: (0, 0), pipeline_mode=once)
    col0 = pl.BlockSpec((1, B_WIDTH, MIX_TM), lambda i: (0, 0, 0), pipeline_mode=once)
    nxt = lambda i: jnp.minimum(i + 1, n - 1)
    rown = pl.BlockSpec((MIX_TM, D_MODEL), lambda i: (nxt(i), 0))
    coln = pl.BlockSpec((1, B_WIDTH, MIX_TM), lambda i: (nxt(i) // nsb, 0, nxt(i) % nsb))
    return pl.pallas_call(
        _mixffn_kernel,
        grid=(n,),
        in_specs=[row0, row0, row0, col0, rown, rown, rown, coln] + [_whole_vmem()] * 6,
        out_specs=pl.BlockSpec((MIX_TM, D_MODEL), lambda i: (i, 0)),
        out_shape=jax.ShapeDtypeStruct((t, D_MODEL), F32),
        scratch_shapes=[pltpu.VMEM((MIX_TM, D_FF), BF16), pltpu.VMEM((MIX_TM, D_MODEL), BF16),
                        pltpu.VMEM((MIX_TM, FFN_LEAD), F32), pltpu.VMEM((MIX_TM, D_MODEL), F32)],
        compiler_params=_params(1),
        name="mixffn",
    )(x1, ma, sgb, ybt, x1, ma, sgb, ybt, pb, wo, gain, wg, wu, wd)


def kernel(x, positions, ffn1_norm, ffn1_w_gate, ffn1_w_up, ffn1_w_down, mix_norm, w_in, a_ln_gain, a_ln_bias, a_w_s, a_b_s, a_w_proj, b_q_norm, b_k_norm, b_lambda_q1, b_lambda_k1, b_lambda_q2, b_lambda_k2, b_subln, b_w_proj, w_out, ffn2_norm, ffn2_w_gate, ffn2_w_up, ffn2_w_down):
    bsz, seq, _ = x.shape
    depth = w_in.shape[0]
    t = bsz * seq
    xt = x.reshape(t, D_MODEL)
    pos = positions.reshape(t // PROJ_TM, 1, PROJ_TM)
    freq = (ROPE_THETA ** (-jnp.arange(0, ROT_DIM, 2, dtype=F32) / ROT_DIM))[:, None]

    for l in range(depth):
        lam_init = 0.8 - 0.6 * math.exp(-0.3 * l)
        row = lambda a: a[l][None, :].astype(F32)
        later = (w_in[l], a_w_proj[l], b_w_proj[l], w_out[l], ffn2_w_gate[l], ffn2_w_up[l], ffn2_w_down[l])
        xt, win_b, pa_b, pb_b, wo_b, wg2_b, wu2_b, wd2_b = _ffn(
            xt, row(ffn1_norm), ffn1_w_gate[l].astype(BF16), ffn1_w_up[l].astype(BF16),
            ffn1_w_down[l].astype(BF16), later)
        ws_pair = a_w_s[l].reshape(A_GROUPS // 2, 2, CHUNK, CHUNK).transpose(0, 2, 1, 3)
        ws_pair = ws_pair.reshape(A_GROUPS // 2, CHUNK, 2 * CHUNK).astype(BF16)
        bs_full = jnp.repeat(a_b_s[l].T, A_GROUP_DIM, axis=1).astype(F32)
        gain_rows = lambda a, scale: jnp.broadcast_to(
            jnp.tile(a[l].astype(F32) * scale, B_WIDTH // B_QK_DIM)[:, None], (B_WIDTH, LANES))
        ma, sgb, qt, k, vt = _proj(
            xt, pos, row(mix_norm), win_b, row(a_ln_gain), row(a_ln_bias), ws_pair, bs_full,
            pa_b, gain_rows(b_q_norm, B_QK_DIM ** -0.5 * LOG2E), gain_rows(b_k_norm, 1.0), freq, bsz)
        lamp = jnp.stack([b_lambda_q1[l], b_lambda_k1[l], b_lambda_q2[l], b_lambda_k2[l]]).astype(F32)
        subg = jnp.broadcast_to(b_subln[l].astype(F32)[:, None], (B_V_DIM, LANES))
        ybt = _attn(lamp, qt, k.reshape(bsz, seq, B_WIDTH), vt, subg, lam_init)
        xt = _mixffn(xt, ma, sgb, ybt, pb_b, wo_b, row(ffn2_norm), wg2_b, wu2_b, wd2_b)
    return xt.reshape(bsz, seq, D_MODEL)
```
